```python
import jax, jax.numpy as jnp
from jax import lax
import numpy as np

D_MODEL = 1024
BATCH = 8
SEQ = 4096
DEPTH = 4

N_MIXERS = 2
N_SSD_LAYERS = (DEPTH + 1) // 2
N_SB_LAYERS = DEPTH // 2
NORM_EPS = 1e-6

SSD_EXPAND = 2
SSD_D_INNER = SSD_EXPAND * D_MODEL
SSD_HEAD_DIM = 64
SSD_HEADS = SSD_D_INNER // SSD_HEAD_DIM
SSD_GROUPS = 8
SSD_HEADS_PER_GROUP = SSD_HEADS // SSD_GROUPS
SSD_STATE = 128
SSD_CONV = 4
SSD_CHUNK = 128
SSD_CONV_DIM = SSD_D_INNER + 2 * SSD_GROUPS * SSD_STATE
SSD_IN_DIM = SSD_D_INNER + SSD_CONV_DIM + SSD_HEADS
SSD_DT_MIN = 1e-3
SSD_DT_MAX = 1e-1

SB_HEADS = 16
SB_HEAD_DIM = D_MODEL // SB_HEADS
SB_Q_BLOCK = 128

FFN_D_FF = 2816
FFN_CONV = 3

kernel_name = "hybrid_ssd_stickbreaking_convffn"


def rms_norm(x, g):
    xf = x.astype(jnp.float32)
    y = xf * lax.rsqrt(jnp.mean(xf * xf, axis=-1, keepdims=True) + NORM_EPS)
    return (y * g.astype(jnp.float32)).astype(x.dtype)


def causal_dwconv(u, w, b):
    width = w.shape[0]
    s = u.shape[1]
    up = jnp.pad(u, ((0, 0), (width - 1, 0), (0, 0)))
    return b + sum(w[k] * up[:, k:k + s] for k in range(width))


def ssd_chunked_scan(xdt, a, bm, cm):
    bsz, s = xdt.shape[:2]
    L, G, HG, P, N = SSD_CHUNK, SSD_GROUPS, SSD_HEADS_PER_GROUP, SSD_HEAD_DIM, SSD_STATE
    nc = s // L

    def to_chunks(t):
        t = t.reshape((bsz, nc, L) + t.shape[2:])
        return jnp.moveaxis(t, 1, 0)

    xc = to_chunks(xdt.reshape(bsz, s, G, HG, P))
    ac = to_chunks(a.reshape(bsz, s, G, HG))
    bc = to_chunks(bm)
    cc = to_chunks(cm)
    causal = jnp.tril(jnp.ones((L, L), dtype=bool))[None, :, :, None, None]

    def step(state, inp):
        x_c, a_c, b_c, c_c = inp
        acum = jnp.cumsum(a_c, axis=1)
        seg = acum[:, :, None] - acum[:, None, :]
        decay = jnp.exp(jnp.where(causal, seg, -jnp.inf))
        cb = jnp.einsum('btgn,bsgn->btsg', c_c, b_c)
        scores = cb[..., None] * decay
        y_diag = jnp.einsum('btsgh,bsghp->btghp', scores, x_c)
        y_off = jnp.einsum('btgn,bghpn->btghp', c_c, state) * jnp.exp(acum)[..., None]
        last = acum[:, -1]
        w_s = jnp.exp(last[:, None] - acum)
        new_state = (state * jnp.exp(last)[..., None, None]
                     + jnp.einsum('bsgn,bsghp->bghpn', b_c, x_c * w_s[..., None]))
        return new_state, (y_diag + y_off).astype(jnp.float32)

    state0 = jnp.zeros((bsz, G, HG, P, N), jnp.float32)
    _, ys = lax.scan(step, state0, (xc, ac, bc, cc))
    return jnp.moveaxis(ys, 0, 1).reshape(bsz, s, SSD_HEADS, P)


def ssd_mixer(h, w_in, conv_w, conv_b, dt_bias, a_log, d_skip, norm_g, w_out):
    bsz, s, _ = h.shape
    proj = h @ w_in
    z, xbc, dt = jnp.split(proj, [SSD_D_INNER, SSD_D_INNER + SSD_CONV_DIM], axis=-1)
    xbc = jax.nn.silu(causal_dwconv(xbc, conv_w, conv_b))
    xs, bm, cm = jnp.split(xbc, [SSD_D_INNER, SSD_D_INNER + SSD_GROUPS * SSD_STATE], axis=-1)
    xs = xs.reshape(bsz, s, SSD_HEADS, SSD_HEAD_DIM)
    bm = bm.reshape(bsz, s, SSD_GROUPS, SSD_STATE)
    cm = cm.reshape(bsz, s, SSD_GROUPS, SSD_STATE)
    dt = jax.nn.softplus(dt.astype(jnp.float32) + dt_bias.astype(jnp.float32))
    a = -jnp.exp(a_log.astype(jnp.float32)) * dt
    y = ssd_chunked_scan(xs * dt[..., None], a, bm, cm)
    y = y + d_skip[:, None] * xs
    y = y.reshape(bsz, s, SSD_D_INNER).astype(h.dtype)
    y = rms_norm(y * jax.nn.silu(z), norm_g)
    return y @ w_out


def stick_breaking_mixer(h, w_qkv, w_out):
    bsz, s, _ = h.shape
    qkv = (h @ w_qkv).reshape(bsz, s, 3, SB_HEADS, SB_HEAD_DIM)
    q = jnp.moveaxis(qkv[:, :, 0], 1, 2)
    k = jnp.moveaxis(qkv[:, :, 1], 1, 2)
    v = jnp.moveaxis(qkv[:, :, 2], 1, 2)
    scale = SB_HEAD_DIM ** -0.5
    outs = []
    for blk in range(s // SB_Q_BLOCK):
        q0 = blk * SB_Q_BLOCK
        kv_end = q0 + SB_Q_BLOCK
        logits = jnp.einsum('bhqd,bhkd->bhqk', q[:, :, q0:kv_end], k[:, :, :kv_end]).astype(jnp.float32) * scale
        qpos = q0 + jnp.arange(SB_Q_BLOCK)[:, None]
        kpos = jnp.arange(kv_end)[None, :]
        strict = kpos < qpos
        log_beta = jax.nn.log_sigmoid(logits)
        log_fail = jnp.where(strict, jax.nn.log_sigmoid(-logits), 0.0)
        suffix = lax.cumsum(log_fail, axis=3, reverse=True) - log_fail
        weights = jnp.where(strict, jnp.exp(log_beta + suffix), 0.0)
        outs.append(jnp.einsum('bhqk,bhkd->bhqd', weights.astype(v.dtype), v[:, :, :kv_end]))
    o = jnp.concatenate(outs, axis=2)
    o = jnp.moveaxis(o, 1, 2).reshape(bsz, s, D_MODEL)
    return o @ w_out


def conv_ffn(h, w_in, conv_w, conv_b, w_out):
    u = causal_dwconv(h @ w_in, conv_w, conv_b)
    gate, up = jnp.split(u, 2, axis=-1)
    return (jax.nn.silu(gate) * up) @ w_out


def setup_inputs(seed: int = 0) -> dict:
    key = jax.random.key(seed)
    ks = jax.random.split(key, 24)
    f32 = jnp.float32
    out_scale = (2.0 * DEPTH) ** -0.5

    def nrm(k, shape, scale):
        return jax.random.normal(k, shape, f32) * scale

    x = jax.random.normal(ks[0], (BATCH, SEQ, D_MODEL), f32)
    mix_norm = 1.0 + nrm(ks[1], (DEPTH, D_MODEL), 0.02)
    ffn_norm = 1.0 + nrm(ks[2], (DEPTH, D_MODEL), 0.02)
    final_norm = 1.0 + nrm(ks[3], (D_MODEL,), 0.02)

    ssd_w_in = nrm(ks[4], (N_SSD_LAYERS, D_MODEL, SSD_IN_DIM), D_MODEL ** -0.5)
    ssd_conv_w = nrm(ks[5], (N_SSD_LAYERS, SSD_CONV, SSD_CONV_DIM), SSD_CONV ** -0.5)
    ssd_conv_b = nrm(ks[6], (N_SSD_LAYERS, SSD_CONV_DIM), 0.02)
    u = jax.random.uniform(ks[7], (N_SSD_LAYERS, SSD_HEADS), f32)
    dt0 = jnp.exp(u * (np.log(SSD_DT_MAX) - np.log(SSD_DT_MIN)) + np.log(SSD_DT_MIN))
    ssd_dt_bias = dt0 + jnp.log(-jnp.expm1(-dt0))
    ssd_a_log = jnp.log(jax.random.uniform(ks[8], (N_SSD_LAYERS, SSD_HEADS), f32, 1.0, 16.0))
    ssd_d = 1.0 + nrm(ks[9], (N_SSD_LAYERS, SSD_HEADS), 0.1)
    ssd_norm = 1.0 + nrm(ks[10], (N_SSD_LAYERS, SSD_D_INNER), 0.02)
    ssd_w_out = nrm(ks[11], (N_SSD_LAYERS, SSD_D_INNER, D_MODEL), SSD_D_INNER ** -0.5 * out_scale)

    sb_w_qkv = nrm(ks[12], (N_SB_LAYERS, D_MODEL, 3 * D_MODEL), D_MODEL ** -0.5)
    sb_w_out = nrm(ks[13], (N_SB_LAYERS, D_MODEL, D_MODEL), D_MODEL ** -0.5 * out_scale)

    ffn_w_in = nrm(ks[14], (DEPTH, D_MODEL, 2 * FFN_D_FF), D_MODEL ** -0.5)
    ffn_conv_w = nrm(ks[15], (DEPTH, FFN_CONV, 2 * FFN_D_FF), FFN_CONV ** -0.5)
    ffn_conv_b = nrm(ks[16], (DEPTH, 2 * FFN_D_FF), 0.02)
    ffn_w_out = nrm(ks[17], (DEPTH, FFN_D_FF, D_MODEL), FFN_D_FF ** -0.5 * out_scale)

    return {"x": x, "mix_norm": mix_norm, "ffn_norm": ffn_norm, "final_norm": final_norm,
            "ssd_w_in": ssd_w_in, "ssd_conv_w": ssd_conv_w, "ssd_conv_b": ssd_conv_b,
            "ssd_dt_bias": ssd_dt_bias, "ssd_a_log": ssd_a_log, "ssd_d": ssd_d,
            "ssd_norm": ssd_norm, "ssd_w_out": ssd_w_out,
            "sb_w_qkv": sb_w_qkv, "sb_w_out": sb_w_out,
            "ffn_w_in": ffn_w_in, "ffn_conv_w": ffn_conv_w, "ffn_conv_b": ffn_conv_b,
            "ffn_w_out": ffn_w_out}


def reference(x, mix_norm, ffn_norm, final_norm,
              ssd_w_in, ssd_conv_w, ssd_conv_b, ssd_dt_bias, ssd_a_log, ssd_d, ssd_norm, ssd_w_out,
              sb_w_qkv, sb_w_out,
              ffn_w_in, ffn_conv_w, ffn_conv_b, ffn_w_out):
    for i in range(DEPTH):
        h = rms_norm(x, mix_norm[i])
        j = i // N_MIXERS
        if i % N_MIXERS == 0:
            x = x + ssd_mixer(h, ssd_w_in[j], ssd_conv_w[j], ssd_conv_b[j], ssd_dt_bias[j],
                              ssd_a_log[j], ssd_d[j], ssd_norm[j], ssd_w_out[j])
        else:
            x = x + stick_breaking_mixer(h, sb_w_qkv[j], sb_w_out[j])
        x = x + conv_ffn(rms_norm(x, ffn_norm[i]), ffn_w_in[i], ffn_conv_w[i], ffn_conv_b[i], ffn_w_out[i])
    return rms_norm(x, final_norm)
```

```python
import functools
import math

import jax
import jax.numpy as jnp
from jax import lax
from jax.experimental import pallas as pl
from jax.experimental.pallas import tpu as pltpu

F32 = jnp.float32
BF16 = jnp.bfloat16

D_MODEL = 1024
DEPTH = 4
NORM_EPS = 1e-6

SSD_D_INNER = 2048
SSD_HEAD_DIM = 64
SSD_HEADS = 32
SSD_GROUPS = 8
SSD_HEADS_PER_GROUP = 4
SSD_STATE = 128
SSD_CHUNK = 128
SSD_CONV_DIM = SSD_D_INNER + 2 * SSD_GROUPS * SSD_STATE
SSD_GROUP_WIDTH = SSD_HEADS_PER_GROUP * SSD_HEAD_DIM

SB_HEADS = 16
SB_HEAD_DIM = 64
SB_BLOCK = 256

FFN_D_FF = 2816

LANES = 128
SUBLANES = 8
VMEM_LIMIT_BYTES = 56 * 1024 * 1024


def _params(*semantics):
    return pltpu.CompilerParams(dimension_semantics=semantics, vmem_limit_bytes=VMEM_LIMIT_BYTES)


def _sigmoid(v):
    return 1.0 / (1.0 + jnp.exp(-v))


def _rms_scale(v):
    return lax.rsqrt(jnp.mean(v * v, axis=-1, keepdims=True) + NORM_EPS)


def _rmsnorm_kernel(x_ref, g_ref, o_ref):
    x = x_ref[...]
    o_ref[...] = (x * _rms_scale(x) * g_ref[...]).astype(o_ref.dtype)


def _rmsnorm(x, g, tm=1024):
    t, d = x.shape
    return pl.pallas_call(
        _rmsnorm_kernel,
        grid=(t // tm,),
        in_specs=[pl.BlockSpec((tm, d), lambda i: (i, 0)), pl.BlockSpec((1, d), lambda i: (0, 0))],
        out_specs=pl.BlockSpec((tm, d), lambda i: (i, 0)),
        out_shape=jax.ShapeDtypeStruct((t, d), BF16),
        compiler_params=_params("arbitrary"),
        name="rmsnorm",
    )(x, g.reshape(1, d))


def _mm_kernel(h_ref, w_ref, o_ref):
    o_ref[...] = jnp.dot(h_ref[...], w_ref[...], preferred_element_type=F32).astype(o_ref.dtype)


def _mm(h, w, out_dtype, tm, tn, name):
    t, k = h.shape
    n = w.shape[1]
    return pl.pallas_call(
        _mm_kernel,
        grid=(n // tn, t // tm),
        in_specs=[pl.BlockSpec((tm, k), lambda j, i: (i, 0)), pl.BlockSpec((k, tn), lambda j, i: (0, j))],
        out_specs=pl.BlockSpec((tm, tn), lambda j, i: (i, j)),
        out_shape=jax.ShapeDtypeStruct((t, n), out_dtype),
        compiler_params=_params("arbitrary", "arbitrary"),
        name=name,
    )(h, w)


def _shift_rows(u, prev, j):
    rolled = pltpu.roll(u, j, axis=0)
    prev_rolled = pltpu.roll(prev, j, axis=0)
    row = lax.broadcasted_iota(jnp.int32, prev.shape, 0)
    top = jnp.where(row < j, prev_rolled, rolled[:SUBLANES])
    return jnp.concatenate([top, rolled[SUBLANES:]], axis=0)


def _causal_conv(u, prev, cw, cb):
    width = cw.shape[0]
    y = cb + cw[width - 1:width, :] * u
    for j in range(1, width):
        y = y + cw[width - 1 - j:width - j, :] * _shift_rows(u, prev, j)
    return y


def _mm_conv_silu_kernel(h_ref, w_ref, cw_ref, cb_ref, o_ref, prev_ref, *, tiles_per_seq):
    @pl.when(pl.program_id(1) % tiles_per_seq == 0)
    def _():
        prev_ref[...] = jnp.zeros_like(prev_ref)

    u = jnp.dot(h_ref[...], w_ref[...], preferred_element_type=F32)
    y = _causal_conv(u, prev_ref[...], cw_ref[...], cb_ref[...])
    prev_ref[...] = u[u.shape[0] - SUBLANES:, :]
    o_ref[...] = (y * _sigmoid(y)).astype(o_ref.dtype)


def _mm_conv_silu(h, w, cw, cb, seq, tm, tn, name):
    t, k = h.shape
    n = w.shape[1]
    width = cw.shape[0]
    return pl.pallas_call(
        functools.partial(_mm_conv_silu_kernel, tiles_per_seq=seq // tm),
        grid=(n // tn, t // tm),
        in_specs=[pl.BlockSpec((tm, k), lambda j, i: (i, 0)),
                  pl.BlockSpec((k, tn), lambda j, i: (0, j)),
                  pl.BlockSpec((width, tn), lambda j, i: (0, j)),
                  pl.BlockSpec((1, tn), lambda j, i: (0, j))],
        out_specs=pl.BlockSpec((tm, tn), lambda j, i: (i, j)),
        out_shape=jax.ShapeDtypeStruct((t, n), BF16),
        scratch_shapes=[pltpu.VMEM((SUBLANES, tn), F32)],
        compiler_params=_params("arbitrary", "arbitrary"),
        name=name,
    )(h, w, cw, cb.reshape(1, n))


def _mm_conv_glu_kernel(h_ref, wg_ref, wu_ref, cwg_ref, cwu_ref, cbg_ref, cbu_ref, o_ref,
                        prev_g_ref, prev_u_ref, *, tiles_per_seq):
    @pl.when(pl.program_id(1) % tiles_per_seq == 0)
    def _():
        prev_g_ref[...] = jnp.zeros_like(prev_g_ref)
        prev_u_ref[...] = jnp.zeros_like(prev_u_ref)

    h = h_ref[...]
    g = jnp.dot(h, wg_ref[...], preferred_element_type=F32)
    yg = _causal_conv(g, prev_g_ref[...], cwg_ref[...], cbg_ref[...])
    prev_g_ref[...] = g[g.shape[0] - SUBLANES:, :]
    u = jnp.dot(h, wu_ref[...], preferred_element_type=F32)
    yu = _causal_conv(u, prev_u_ref[...], cwu_ref[...], cbu_ref[...])
    prev_u_ref[...] = u[u.shape[0] - SUBLANES:, :]
    o_ref[...] = (yg * _sigmoid(yg) * yu).astype(o_ref.dtype)


def _mm_conv_glu(h, w, cw, cb, seq, tm, tn, name):
    t, k = h.shape
    f = w.shape[1] // 2
    width = cw.shape[0]
    nt = f // tn
    cb2 = cb.reshape(1, 2 * f)
    return pl.pallas_call(
        functools.partial(_mm_conv_glu_kernel, tiles_per_seq=seq // tm),
        grid=(nt, t // tm),
        in_specs=[pl.BlockSpec((tm, k), lambda j, i: (i, 0)),
                  pl.BlockSpec((k, tn), lambda j, i: (0, j)),
                  pl.BlockSpec((k, tn), lambda j, i: (0, j + nt)),
                  pl.BlockSpec((width, tn), lambda j, i: (0, j)),
                  pl.BlockSpec((width, tn), lambda j, i: (0, j + nt)),
                  pl.BlockSpec((1, tn), lambda j, i: (0, j)),
                  pl.BlockSpec((1, tn), lambda j, i: (0, j + nt))],
        out_specs=pl.BlockSpec((tm, tn), lambda j, i: (i, j)),
        out_shape=jax.ShapeDtypeStruct((t, f), BF16),
        scratch_shapes=[pltpu.VMEM((SUBLANES, tn), F32), pltpu.VMEM((SUBLANES, tn), F32)],
        compiler_params=_params("arbitrary", "arbitrary"),
        name=name,
    )(h, w, w, cw, cw, cb2, cb2)


def _mm_res_norm_kernel(a_ref, w_ref, x_ref, g_ref, xo_ref, ho_ref):
    xn = x_ref[...] + jnp.dot(a_ref[...], w_ref[...], preferred_element_type=F32)
    xo_ref[...] = xn
    ho_ref[...] = (xn * _rms_scale(xn) * g_ref[...]).astype(ho_ref.dtype)


def _mm_res_final_kernel(a_ref, w_ref, x_ref, g_ref, o_ref):
    xn = x_ref[...] + jnp.dot(a_ref[...], w_ref[...], preferred_element_type=F32)
    o_ref[...] = xn * _rms_scale(xn) * g_ref[...]


def _mm_res_norm(a, w, x, g, final, tm, name):
    t, k = a.shape
    d = w.shape[1]
    row = pl.BlockSpec((tm, d), lambda i: (i, 0))
    in_specs = [pl.BlockSpec((tm, k), lambda i: (i, 0)), pl.BlockSpec((k, d), lambda i: (0, 0)),
                row, pl.BlockSpec((1, d), lambda i: (0, 0))]
    if final:
        return pl.pallas_call(
            _mm_res_final_kernel, grid=(t // tm,), in_specs=in_specs, out_specs=row,
            out_shape=jax.ShapeDtypeStruct((t, d), F32),
            compiler_params=_params("arbitrary"), name=name,
        )(a, w, x, g.reshape(1, d))
    return pl.pallas_call(
        _mm_res_norm_kernel, grid=(t // tm,), in_specs=in_specs, out_specs=[row, row],
        out_shape=[jax.ShapeDtypeStruct((t, d), F32), jax.ShapeDtypeStruct((t, d), BF16)],
        compiler_params=_params("arbitrary"), name=name,
    )(a, w, x, g.reshape(1, d))


def _split3(v):
    a = v.astype(BF16)
    r = v - a.astype(F32)
    b = r.astype(BF16)
    c = (r - b.astype(F32)).astype(BF16)
    return a, b, c


def _expand_heads(v, h0, nheads):
    rows = v.shape[0]
    lane = lax.broadcasted_iota(jnp.int32, (rows, LANES), 1)
    pairs = []
    for p in range(nheads // 2):
        lo = jnp.broadcast_to(v[:, h0 + 2 * p:h0 + 2 * p + 1], (rows, LANES))
        hi = jnp.broadcast_to(v[:, h0 + 2 * p + 1:h0 + 2 * p + 2], (rows, LANES))
        pairs.append(jnp.where(lane < SSD_HEAD_DIM, lo, hi))
    return jnp.concatenate(pairs, axis=1)


def _ssd_kernel(z_ref, xbc_ref, dt_ref, dtb_ref, alog_ref, dskip_ref, ng_ref, o_ref, state_ref, y_ref):
    L = SSD_CHUNK

    @pl.when(pl.program_id(1) == 0)
    def _():
        state_ref[...] = jnp.zeros_like(state_ref)

    pre = dt_ref[...] + dtb_ref[...]
    dt = jnp.maximum(pre, 0.0) + jnp.log1p(jnp.exp(-jnp.abs(pre)))
    a = -jnp.exp(alog_ref[...]) * dt
    row = lax.broadcasted_iota(jnp.int32, (L, L), 0)
    col = lax.broadcasted_iota(jnp.int32, (L, L), 1)
    causal = row >= col
    tri = jnp.where(causal, 1.0, 0.0).astype(BF16)
    a1, a2, a3 = _split3(a)
    acum = (jnp.dot(tri, a1, preferred_element_type=F32) + jnp.dot(tri, a2, preferred_element_type=F32)
            + jnp.dot(tri, a3, preferred_element_type=F32))
    acum_t = acum.T
    dt_t = dt.T
    last = acum[L - 1:L, :]
    exp_acum = jnp.exp(acum)
    w_dt = jnp.exp(last - acum) * dt
    exp_last = jnp.exp(last)
    lane = lax.broadcasted_iota(jnp.int32, (L, LANES), 1)
    lo_lanes = lane < SSD_HEAD_DIM

    for g in range(SSD_GROUPS):
        b_g = xbc_ref[:, SSD_D_INNER + g * SSD_STATE:SSD_D_INNER + (g + 1) * SSD_STATE]
        c_off = SSD_D_INNER + SSD_GROUPS * SSD_STATE
        c_g = xbc_ref[:, c_off + g * SSD_STATE:c_off + (g + 1) * SSD_STATE]
        cb = lax.dot_general(c_g, b_g, (((1,), (1,)), ((), ())), preferred_element_type=F32)
        x0 = g * SSD_GROUP_WIDTH
        h0 = g * SSD_HEADS_PER_GROUP
        y_diag = []
        for p in range(SSD_HEADS_PER_GROUP // 2):
            scores = []
            for hh in (h0 + 2 * p, h0 + 2 * p + 1):
                seg = acum[:, hh:hh + 1] - acum_t[hh:hh + 1, :]
                decay = jnp.exp(jnp.where(causal, seg, -jnp.inf))
                scores.append((cb * decay * dt_t[hh:hh + 1, :]).astype(BF16))
            xs = xbc_ref[:, x0 + p * LANES:x0 + (p + 1) * LANES]
            zero = jnp.zeros_like(xs)
            xs_split = jnp.concatenate([jnp.where(lo_lanes, xs, zero), jnp.where(lo_lanes, zero, xs)], axis=0)
            y_diag.append(jnp.dot(jnp.concatenate(scores, axis=1), xs_split, preferred_element_type=F32))
        y_diag = jnp.concatenate(y_diag, axis=1)

        xs_g = xbc_ref[:, x0:x0 + SSD_GROUP_WIDTH].astype(F32)
        state = state_ref[g]
        y_off = (jnp.dot(c_g, state.astype(BF16), preferred_element_type=F32)
                 * _expand_heads(exp_acum, h0, SSD_HEADS_PER_GROUP))
        xw = (xs_g * _expand_heads(w_dt, h0, SSD_HEADS_PER_GROUP)).astype(BF16)
        state_ref[g] = (state * _expand_heads(exp_last, h0, SSD_HEADS_PER_GROUP)
                        + lax.dot_general(b_g, xw, (((0,), (0,)), ((), ())), preferred_element_type=F32))
        y_ref[:, x0:x0 + SSD_GROUP_WIDTH] = y_diag + y_off + dskip_ref[:, x0:x0 + SSD_GROUP_WIDTH] * xs_g

    z = z_ref[...].astype(F32)
    gated = y_ref[...] * (z * _sigmoid(z))
    o_ref[...] = (gated * _rms_scale(gated) * ng_ref[...]).astype(o_ref.dtype)


def _ssd_scan(z, xbc, dt_raw, dt_bias, a_log, d_skip, norm_g, batch, seq):
    nc = seq // SSD_CHUNK
    L = SSD_CHUNK
    pad = LANES - SSD_HEADS
    dtb = jnp.pad(dt_bias, (0, pad)).reshape(1, LANES)
    alog = jnp.pad(a_log, (0, pad)).reshape(1, LANES)
    dskip = jnp.repeat(d_skip, SSD_HEAD_DIM).reshape(1, SSD_D_INNER)
    small = lambda n: pl.BlockSpec((1, n), lambda b, c: (0, 0))
    return pl.pallas_call(
        _ssd_kernel,
        grid=(batch, nc),
        in_specs=[pl.BlockSpec((L, SSD_D_INNER), lambda b, c: (b * nc + c, 0)),
                  pl.BlockSpec((L, SSD_CONV_DIM), lambda b, c: (b * nc + c, 0)),
                  pl.BlockSpec((L, LANES), lambda b, c: (b * nc + c, 0)),
                  small(LANES), small(LANES), small(SSD_D_INNER), small(SSD_D_INNER)],
        out_specs=pl.BlockSpec((L, SSD_D_INNER), lambda b, c: (b * nc + c, 0)),
        out_shape=jax.ShapeDtypeStruct((batch * seq, SSD_D_INNER), BF16),
        scratch_shapes=[pltpu.VMEM((SSD_GROUPS, SSD_STATE, SSD_GROUP_WIDTH), F32),
                        pltpu.VMEM((L, SSD_D_INNER), F32)],
        compiler_params=_params("arbitrary", "arbitrary"),
        name="ssd_scan",
    )(z, xbc, dt_raw, dtb, alog, dskip, norm_g.reshape(1, SSD_D_INNER))


def _sb_head_block(qh, kb, carry, suffix_mat, strict):
    tk = kb.shape[0]
    l = lax.dot_general(qh, kb, (((1,), (1,)), ((), ())), preferred_element_type=F32)
    lg = jnp.log2(1.0 + jnp.exp2(-jnp.abs(l)))
    log_beta = jnp.minimum(l, 0.0) - lg
    log_fail = log_beta - l
    if strict is not None:
        log_fail = jnp.where(strict, log_fail, 0.0)
    hi = log_fail.astype(BF16)
    lo = (log_fail - hi.astype(F32)).astype(BF16)
    suffix = []
    for m in reversed(range(tk // LANES)):
        sl = slice(m * LANES, (m + 1) * LANES)
        r = jnp.dot(jnp.concatenate([hi[:, sl], lo[:, sl]], axis=1), suffix_mat, preferred_element_type=F32)
        suffix.append(r[:, :LANES] + carry)
        carry = carry + r[:, LANES:]
    w = jnp.exp2(log_beta + jnp.concatenate(suffix[::-1], axis=1))
    if strict is not None:
        w = jnp.where(strict, w, 0.0)
    return w.astype(BF16), carry


def _sb_kernel(q_ref, k_ref, v_ref, o_ref):
    tq = SB_BLOCK
    i = pl.program_id(2)
    q = q_ref[...]
    zero_q = jnp.zeros_like(q)
    lo_q = lax.broadcasted_iota(jnp.int32, q.shape, 1) < SB_HEAD_DIM
    q_heads = (jnp.where(lo_q, q, zero_q), jnp.where(lo_q, zero_q, q))
    r = lax.broadcasted_iota(jnp.int32, (LANES, 2 * LANES), 0)
    c = lax.broadcasted_iota(jnp.int32, (LANES, 2 * LANES), 1)
    half = jnp.where((r > c) | (c >= LANES), 1.0, 0.0).astype(BF16)
    suffix_mat = jnp.concatenate([half, half], axis=0)
    lo_v = lax.broadcasted_iota(jnp.int32, (tq, LANES), 1) < SB_HEAD_DIM

    def step(j, carries, acc, strict):
        start = pl.multiple_of(j * tq, tq)
        kb = k_ref[pl.ds(start, tq), :]
        vb = v_ref[pl.ds(start, tq), :]
        zero_v = jnp.zeros_like(vb)
        v_split = jnp.concatenate([jnp.where(lo_v, vb, zero_v), jnp.where(lo_v, zero_v, vb)], axis=0)
        ws, new_carries = [], []
        for qh, carry in zip(q_heads, carries):
            w, carry = _sb_head_block(qh, kb, carry, suffix_mat, strict)
            ws.append(w)
            new_carries.append(carry)
        acc = acc + jnp.dot(jnp.concatenate(ws, axis=1), v_split, preferred_element_type=F32)
        return tuple(new_carries), acc

    zeros = jnp.zeros((tq, LANES), F32)
    qpos = lax.broadcasted_iota(jnp.int32, (tq, tq), 0)
    kpos = lax.broadcasted_iota(jnp.int32, (tq, tq), 1)
    carries, acc = step(i, (zeros, zeros), zeros, kpos < qpos)

    def body(jj, state):
        carries, acc = state
        return step(i - 1 - jj, carries, acc, None)

    _, acc = lax.fori_loop(0, i, body, (carries, acc))
    o_ref[...] = acc.astype(o_ref.dtype)


def _sb_attention(qkv, batch, seq):
    tq = SB_BLOCK
    nq = seq // tq
    npairs = SB_HEADS // 2
    return pl.pallas_call(
        _sb_kernel,
        grid=(batch, npairs, nq),
        in_specs=[pl.BlockSpec((tq, LANES), lambda b, p, i: (b * nq + i, p)),
                  pl.BlockSpec((seq, LANES), lambda b, p, i: (b, npairs + p)),
                  pl.BlockSpec((seq, LANES), lambda b, p, i: (b, 2 * npairs + p))],
        out_specs=pl.BlockSpec((tq, LANES), lambda b, p, i: (b * nq + i, p)),
        out_shape=jax.ShapeDtypeStruct((batch * seq, D_MODEL), BF16),
        compiler_params=_params("arbitrary", "arbitrary", "arbitrary"),
        name="sb_attention",
    )(qkv, qkv, qkv)


def kernel(x, mix_norm, ffn_norm, final_norm, ssd_w_in, ssd_conv_w, ssd_conv_b, ssd_dt_bias, ssd_a_log, ssd_d, ssd_norm, ssd_w_out, sb_w_qkv, sb_w_out, ffn_w_in, ffn_conv_w, ffn_conv_b, ffn_w_out):
    batch, seq, d = x.shape
    xf = x.reshape(batch * seq, d)
    h = _rmsnorm(xf, mix_norm[0])
    out = None
    for i in range(DEPTH):
        j = i // 2
        if i % 2 == 0:
            w = ssd_w_in[j]
            w_dt = jnp.pad(w[:, SSD_D_INNER + SSD_CONV_DIM:], ((0, 0), (0, LANES - SSD_HEADS)))
            z = _mm(h, w[:, :SSD_D_INNER].astype(BF16), BF16, 1024, 1024, "ssd_z")
            xbc = _mm_conv_silu(h, w[:, SSD_D_INNER:SSD_D_INNER + SSD_CONV_DIM].astype(BF16),
                                ssd_conv_w[j], ssd_conv_b[j], seq, 512, 1024, "ssd_xbc")
            dt_raw = _mm(h, w_dt.astype(BF16), F32, 1024, LANES, "ssd_dt")
            mixed = _ssd_scan(z, xbc, dt_raw, ssd_dt_bias[j], ssd_a_log[j], ssd_d[j], ssd_norm[j], batch, seq)
            w_out = ssd_w_out[j]
        else:
            q_scale = jnp.concatenate([jnp.full((d,), math.log2(math.e) * SB_HEAD_DIM ** -0.5, F32),
                                       jnp.ones((2 * d,), F32)])
            qkv = _mm(h, (sb_w_qkv[j] * q_scale).astype(BF16), BF16, 1024, 1024, "sb_qkv")
            mixed = _sb_attention(qkv, batch, seq)
            w_out = sb_w_out[j]
        xf, h = _mm_res_norm(mixed, w_out.astype(BF16), xf, ffn_norm[i], False, 512, "mixer_out")
        act = _mm_conv_glu(h, ffn_w_in[i].astype(BF16), ffn_conv_w[i], ffn_conv_b[i], seq, 512, FFN_D_FF // 2,
                           "ffn_in")
        if i + 1 < DEPTH:
            xf, h = _mm_res_norm(act, ffn_w_out[i].astype(BF16), xf, mix_norm[i + 1], False, 512, "ffn_out")
        else:
            out = _mm_res_norm(act, ffn_w_out[i].astype(BF16), xf, final_norm, True, 512, "ffn_out_final")
    return out.reshape(batch, seq, d)
```

```python
import functools
import math

import jax
import jax.numpy as jnp
from jax import lax
from jax.experimental import pallas as pl
from jax.experimental.pallas import tpu as pltpu

F32 = jnp.float32
BF16 = jnp.bfloat16

D_MODEL = 1024
DEPTH = 4
NORM_EPS = 1e-6

SSD_D_INNER = 2048
SSD_HEAD_DIM = 64
SSD_HEADS = 32
SSD_GROUPS = 8
SSD_HEADS_PER_GROUP = 4
SSD_STATE = 128
SSD_CHUNK = 128
SSD_CONV_DIM = SSD_D_INNER + 2 * SSD_GROUPS * SSD_STATE
SSD_GROUP_WIDTH = SSD_HEADS_PER_GROUP * SSD_HEAD_DIM

SB_HEADS = 16
SB_HEAD_DIM = 64
SB_BLOCK = 256
SB_PAIRS_PER_STEP = 2
SB_UNDERFLOW_LOG2 = 160.0

FFN_D_FF = 2816

LANES = 128
SUBLANES = 8
VMEM_LIMIT_BYTES = 56 * 1024 * 1024


def _params(*semantics):
    return pltpu.CompilerParams(dimension_semantics=semantics, vmem_limit_bytes=VMEM_LIMIT_BYTES)


def _sigmoid(v):
    return 1.0 / (1.0 + jnp.exp(-v))


def _rms_scale(v):
    return lax.rsqrt(jnp.mean(v * v, axis=-1, keepdims=True) + NORM_EPS)


def _rmsnorm_kernel(x_ref, g_ref, o_ref):
    x = x_ref[...]
    o_ref[...] = (x * _rms_scale(x) * g_ref[...]).astype(o_ref.dtype)


def _rmsnorm(x, g, tm=1024):
    t, d = x.shape
    return pl.pallas_call(
        _rmsnorm_kernel,
        grid=(t // tm,),
        in_specs=[pl.BlockSpec((tm, d), lambda i: (i, 0)), pl.BlockSpec((1, d), lambda i: (0, 0))],
        out_specs=pl.BlockSpec((tm, d), lambda i: (i, 0)),
        out_shape=jax.ShapeDtypeStruct((t, d), BF16),
        compiler_params=_params("arbitrary"),
        name="rmsnorm",
    )(x, g.reshape(1, d))


def _mm_kernel(h_ref, w_ref, o_ref):
    o_ref[...] = jnp.dot(h_ref[...], w_ref[...], preferred_element_type=F32).astype(o_ref.dtype)


def _mm(h, w, out_dtype, tm, tn, name):
    t, k = h.shape
    n = w.shape[1]
    return pl.pallas_call(
        _mm_kernel,
        grid=(n // tn, t // tm),
        in_specs=[pl.BlockSpec((tm, k), lambda j, i: (i, 0)), pl.BlockSpec((k, tn), lambda j, i: (0, j))],
        out_specs=pl.BlockSpec((tm, tn), lambda j, i: (i, j)),
        out_shape=jax.ShapeDtypeStruct((t, n), out_dtype),
        compiler_params=_params("arbitrary", "arbitrary"),
        name=name,
    )(h, w)


def _shift_rows(u, prev, j):
    rolled = pltpu.roll(u, j, axis=0)
    prev_rolled = pltpu.roll(prev, j, axis=0)
    row = lax.broadcasted_iota(jnp.int32, prev.shape, 0)
    top = jnp.where(row < j, prev_rolled, rolled[:SUBLANES])
    return jnp.concatenate([top, rolled[SUBLANES:]], axis=0)


def _causal_conv(u, prev, cw, cb):
    width = cw.shape[0]
    y = cb + cw[width - 1:width, :] * u
    for j in range(1, width):
        y = y + cw[width - 1 - j:width - j, :] * _shift_rows(u, prev, j)
    return y


def _mm_conv_silu_kernel(h_ref, w_ref, cw_ref, cb_ref, o_ref, prev_ref, *, tiles_per_seq):
    @pl.when(pl.program_id(1) % tiles_per_seq == 0)
    def _():
        prev_ref[...] = jnp.zeros_like(prev_ref)

    u = jnp.dot(h_ref[...], w_ref[...], preferred_element_type=F32)
    y = _causal_conv(u, prev_ref[...], cw_ref[...], cb_ref[...])
    prev_ref[...] = u[u.shape[0] - SUBLANES:, :]
    o_ref[...] = (y * _sigmoid(y)).astype(o_ref.dtype)


def _mm_conv_silu(h, w, cw, cb, seq, tm, tn, name):
    t, k = h.shape
    n = w.shape[1]
    width = cw.shape[0]
    return pl.pallas_call(
        functools.partial(_mm_conv_silu_kernel, tiles_per_seq=seq // tm),
        grid=(n // tn, t // tm),
        in_specs=[pl.BlockSpec((tm, k), lambda j, i: (i, 0)),
                  pl.BlockSpec((k, tn), lambda j, i: (0, j)),
                  pl.BlockSpec((width, tn), lambda j, i: (0, j)),
                  pl.BlockSpec((1, tn), lambda j, i: (0, j))],
        out_specs=pl.BlockSpec((tm, tn), lambda j, i: (i, j)),
        out_shape=jax.ShapeDtypeStruct((t, n), BF16),
        scratch_shapes=[pltpu.VMEM((SUBLANES, tn), F32)],
        compiler_params=_params("arbitrary", "arbitrary"),
        name=name,
    )(h, w, cw, cb.reshape(1, n))


def _mm_conv_glu_kernel(h_ref, wg_ref, wu_ref, cwg_ref, cwu_ref, cbg_ref, cbu_ref, o_ref,
                        prev_g_ref, prev_u_ref, *, tiles_per_seq):
    @pl.when(pl.program_id(1) % tiles_per_seq == 0)
    def _():
        prev_g_ref[...] = jnp.zeros_like(prev_g_ref)
        prev_u_ref[...] = jnp.zeros_like(prev_u_ref)

    h = h_ref[...]
    g = jnp.dot(h, wg_ref[...], preferred_element_type=F32)
    yg = _causal_conv(g, prev_g_ref[...], cwg_ref[...], cbg_ref[...])
    prev_g_ref[...] = g[g.shape[0] - SUBLANES:, :]
    u = jnp.dot(h, wu_ref[...], preferred_element_type=F32)
    yu = _causal_conv(u, prev_u_ref[...], cwu_ref[...], cbu_ref[...])
    prev_u_ref[...] = u[u.shape[0] - SUBLANES:, :]
    o_ref[...] = (yg * _sigmoid(yg) * yu).astype(o_ref.dtype)


def _mm_conv_glu(h, w, cw, cb, seq, tm, tn, name):
    t, k = h.shape
    f = w.shape[1] // 2
    width = cw.shape[0]
    nt = f // tn
    cb2 = cb.reshape(1, 2 * f)
    return pl.pallas_call(
        functools.partial(_mm_conv_glu_kernel, tiles_per_seq=seq // tm),
        grid=(nt, t // tm),
        in_specs=[pl.BlockSpec((tm, k), lambda j, i: (i, 0)),
                  pl.BlockSpec((k, tn), lambda j, i: (0, j)),
                  pl.BlockSpec((k, tn), lambda j, i: (0, j + nt)),
                  pl.BlockSpec((width, tn), lambda j, i: (0, j)),
                  pl.BlockSpec((width, tn), lambda j, i: (0, j + nt)),
                  pl.BlockSpec((1, tn), lambda j, i: (0, j)),
                  pl.BlockSpec((1, tn), lambda j, i: (0, j + nt))],
        out_specs=pl.BlockSpec((tm, tn), lambda j, i: (i, j)),
        out_shape=jax.ShapeDtypeStruct((t, f), BF16),
        scratch_shapes=[pltpu.VMEM((SUBLANES, tn), F32), pltpu.VMEM((SUBLANES, tn), F32)],
        compiler_params=_params("arbitrary", "arbitrary"),
        name=name,
    )(h, w, w, cw, cw, cb2, cb2)


def _mm_res_norm_kernel(a_ref, w_ref, x_ref, g_ref, xo_ref, ho_ref):
    xn = x_ref[...] + jnp.dot(a_ref[...], w_ref[...], preferred_element_type=F32)
    xo_ref[...] = xn
    ho_ref[...] = (xn * _rms_scale(xn) * g_ref[...]).astype(ho_ref.dtype)


def _mm_res_final_kernel(a_ref, w_ref, x_ref, g_ref, o_ref):
    xn = x_ref[...] + jnp.dot(a_ref[...], w_ref[...], preferred_element_type=F32)
    o_ref[...] = xn * _rms_scale(xn) * g_ref[...]


def _mm_res_norm(a, w, x, g, final, tm, name):
    t, k = a.shape
    d = w.shape[1]
    row = pl.BlockSpec((tm, d), lambda i: (i, 0))
    in_specs = [pl.BlockSpec((tm, k), lambda i: (i, 0)), pl.BlockSpec((k, d), lambda i: (0, 0)),
                row, pl.BlockSpec((1, d), lambda i: (0, 0))]
    if final:
        return pl.pallas_call(
            _mm_res_final_kernel, grid=(t // tm,), in_specs=in_specs, out_specs=row,
            out_shape=jax.ShapeDtypeStruct((t, d), F32),
            compiler_params=_params("arbitrary"), name=name,
        )(a, w, x, g.reshape(1, d))
    return pl.pallas_call(
        _mm_res_norm_kernel, grid=(t // tm,), in_specs=in_specs, out_specs=[row, row],
        out_shape=[jax.ShapeDtypeStruct((t, d), F32), jax.ShapeDtypeStruct((t, d), BF16)],
        compiler_params=_params("arbitrary"), name=name,
    )(a, w, x, g.reshape(1, d))


def _split3(v):
    a = v.astype(BF16)
    r = v - a.astype(F32)
    b = r.astype(BF16)
    c = (r - b.astype(F32)).astype(BF16)
    return a, b, c


def _expand_heads(v, h0, nheads):
    rows = v.shape[0]
    lane = lax.broadcasted_iota(jnp.int32, (rows, LANES), 1)
    pairs = []
    for p in range(nheads // 2):
        lo = jnp.broadcast_to(v[:, h0 + 2 * p:h0 + 2 * p + 1], (rows, LANES))
        hi = jnp.broadcast_to(v[:, h0 + 2 * p + 1:h0 + 2 * p + 2], (rows, LANES))
        pairs.append(jnp.where(lane < SSD_HEAD_DIM, lo, hi))
    return jnp.concatenate(pairs, axis=1)


def _ssd_kernel(z_ref, xbc_ref, dt_ref, dtb_ref, alog_ref, dskip_ref, ng_ref, o_ref, state_ref, y_ref):
    L = SSD_CHUNK

    @pl.when(pl.program_id(1) == 0)
    def _():
        state_ref[...] = jnp.zeros_like(state_ref)

    pre = dt_ref[...] + dtb_ref[...]
    dt = jnp.maximum(pre, 0.0) + jnp.log1p(jnp.exp(-jnp.abs(pre)))
    a = -jnp.exp(alog_ref[...]) * dt
    row = lax.broadcasted_iota(jnp.int32, (L, L), 0)
    col = lax.broadcasted_iota(jnp.int32, (L, L), 1)
    causal = row >= col
    tri = jnp.where(causal, 1.0, 0.0).astype(BF16)
    a1, a2, a3 = _split3(a)
    acum = (jnp.dot(tri, a1, preferred_element_type=F32) + jnp.dot(tri, a2, preferred_element_type=F32)
            + jnp.dot(tri, a3, preferred_element_type=F32))
    acum_t = acum.T
    dt_t = dt.T
    last = acum[L - 1:L, :]
    exp_acum = jnp.exp(acum)
    w_dt = jnp.exp(last - acum) * dt
    exp_last = jnp.exp(last)
    lane = lax.broadcasted_iota(jnp.int32, (L, LANES), 1)
    lo_lanes = lane < SSD_HEAD_DIM

    for g in range(SSD_GROUPS):
        b_g = xbc_ref[:, SSD_D_INNER + g * SSD_STATE:SSD_D_INNER + (g + 1) * SSD_STATE]
        c_off = SSD_D_INNER + SSD_GROUPS * SSD_STATE
        c_g = xbc_ref[:, c_off + g * SSD_STATE:c_off + (g + 1) * SSD_STATE]
        cb = lax.dot_general(c_g, b_g, (((1,), (1,)), ((), ())), preferred_element_type=F32)
        x0 = g * SSD_GROUP_WIDTH
        h0 = g * SSD_HEADS_PER_GROUP
        y_diag = []
        for p in range(SSD_HEADS_PER_GROUP // 2):
            scores = []
            for hh in (h0 + 2 * p, h0 + 2 * p + 1):
                seg = acum[:, hh:hh + 1] - acum_t[hh:hh + 1, :]
                decay = jnp.exp(jnp.where(causal, seg, -jnp.inf))
                scores.append((cb * decay * dt_t[hh:hh + 1, :]).astype(BF16))
            xs = xbc_ref[:, x0 + p * LANES:x0 + (p + 1) * LANES]
            zero = jnp.zeros_like(xs)
            xs_split = jnp.concatenate([jnp.where(lo_lanes, xs, zero), jnp.where(lo_lanes, zero, xs)], axis=0)
            y_diag.append(jnp.dot(jnp.concatenate(scores, axis=1), xs_split, preferred_element_type=F32))
        y_diag = jnp.concatenate(y_diag, axis=1)

        xs_g = xbc_ref[:, x0:x0 + SSD_GROUP_WIDTH].astype(F32)
        state = state_ref[g]
        y_off = (jnp.dot(c_g, state.astype(BF16), preferred_element_type=F32)
                 * _expand_heads(exp_acum, h0, SSD_HEADS_PER_GROUP))
        xw = (xs_g * _expand_heads(w_dt, h0, SSD_HEADS_PER_GROUP)).astype(BF16)
        state_ref[g] = (state * _expand_heads(exp_last, h0, SSD_HEADS_PER_GROUP)
                        + lax.dot_general(b_g, xw, (((0,), (0,)), ((), ())), preferred_element_type=F32))
        y_ref[:, x0:x0 + SSD_GROUP_WIDTH] = y_diag + y_off + dskip_ref[:, x0:x0 + SSD_GROUP_WIDTH] * xs_g

    z = z_ref[...].astype(F32)
    gated = y_ref[...] * (z * _sigmoid(z))
    o_ref[...] = (gated * _rms_scale(gated) * ng_ref[...]).astype(o_ref.dtype)


def _ssd_scan(z, xbc, dt_raw, dt_bias, a_log, d_skip, norm_g, batch, seq):
    nc = seq // SSD_CHUNK
    L = SSD_CHUNK
    pad = LANES - SSD_HEADS
    dtb = jnp.pad(dt_bias, (0, pad)).reshape(1, LANES)
    alog = jnp.pad(a_log, (0, pad)).reshape(1, LANES)
    dskip = jnp.repeat(d_skip, SSD_HEAD_DIM).reshape(1, SSD_D_INNER)
    small = lambda n: pl.BlockSpec((1, n), lambda b, c: (0, 0))
    return pl.pallas_call(
        _ssd_kernel,
        grid=(batch, nc),
        in_specs=[pl.BlockSpec((L, SSD_D_INNER), lambda b, c: (b * nc + c, 0)),
                  pl.BlockSpec((L, SSD_CONV_DIM), lambda b, c: (b * nc + c, 0)),
                  pl.BlockSpec((L, LANES), lambda b, c: (b * nc + c, 0)),
                  small(LANES), small(LANES), small(SSD_D_INNER), small(SSD_D_INNER)],
        out_specs=pl.BlockSpec((L, SSD_D_INNER), lambda b, c: (b * nc + c, 0)),
        out_shape=jax.ShapeDtypeStruct((batch * seq, SSD_D_INNER), BF16),
        scratch_shapes=[pltpu.VMEM((SSD_GROUPS, SSD_STATE, SSD_GROUP_WIDTH), F32),
                        pltpu.VMEM((L, SSD_D_INNER), F32)],
        compiler_params=_params("arbitrary", "arbitrary"),
        name="ssd_scan",
    )(z, xbc, dt_raw, dtb, alog, dskip, norm_g.reshape(1, SSD_D_INNER))


def _sb_head_block(qh, kb, carry, suffix_mat, strict):
    tk = kb.shape[0]
    l = lax.dot_general(qh, kb, (((1,), (1,)), ((), ())), preferred_element_type=F32)
    neg_abs = lax.bitcast_convert_type(lax.bitcast_convert_type(l, jnp.uint32) | jnp.uint32(0x80000000), F32)
    lg = jnp.log2(1.0 + jnp.exp2(neg_abs))
    log_beta = jnp.minimum(l, 0.0) - lg
    log_fail = log_beta - l
    if strict is not None:
        log_fail = jnp.where(strict, log_fail, 0.0)
    hi = log_fail.astype(BF16)
    lo = (log_fail - hi.astype(F32)).astype(BF16)
    suffix = []
    for m in reversed(range(tk // LANES)):
        sl = slice(m * LANES, (m + 1) * LANES)
        r = jnp.dot(jnp.concatenate([hi[:, sl], lo[:, sl]], axis=1), suffix_mat, preferred_element_type=F32)
        suffix.append(r[:, :LANES] + carry)
        carry = carry + r[:, LANES:]
    w = jnp.exp2(log_beta + jnp.concatenate(suffix[::-1], axis=1))
    if strict is not None:
        w = jnp.where(strict, w, 0.0)
    return w.astype(BF16), carry


def _sb_kernel(q_ref, k_ref, v_ref, o_ref, carry_ref, acc_ref):
    tq = SB_BLOCK
    i = pl.program_id(2)
    npairs = q_ref.shape[1] // LANES
    r = lax.broadcasted_iota(jnp.int32, (LANES, 2 * LANES), 0)
    c = lax.broadcasted_iota(jnp.int32, (LANES, 2 * LANES), 1)
    half = jnp.where((r > c) | (c >= LANES), 1.0, 0.0).astype(BF16)
    suffix_mat = jnp.concatenate([half, half], axis=0)
    lo_lanes = lax.broadcasted_iota(jnp.int32, (tq, LANES), 1) < SB_HEAD_DIM

    def step(j, strict):
        start = pl.multiple_of(j * tq, tq)
        top = None
        for p in range(npairs):
            lanes = slice(p * LANES, (p + 1) * LANES)
            q = q_ref[:, lanes]
            kb = k_ref[pl.ds(start, tq), lanes]
            vb = v_ref[pl.ds(start, tq), lanes]
            zero = jnp.zeros_like(q)
            v_split = jnp.concatenate([jnp.where(lo_lanes, vb, zero), jnp.where(lo_lanes, zero, vb)], axis=0)
            ws = []
            for hh, qh in enumerate((jnp.where(lo_lanes, q, zero), jnp.where(lo_lanes, zero, q))):
                w, carry = _sb_head_block(qh, kb, carry_ref[2 * p + hh], suffix_mat, strict)
                carry_ref[2 * p + hh] = carry
                ws.append(w)
                top = carry if top is None else jnp.maximum(top, carry)
            acc_ref[p] += jnp.dot(jnp.concatenate(ws, axis=1), v_split, preferred_element_type=F32)
        return jnp.max(top)

    carry_ref[...] = jnp.zeros_like(carry_ref)
    acc_ref[...] = jnp.zeros_like(acc_ref)
    qpos = lax.broadcasted_iota(jnp.int32, (tq, tq), 0)
    kpos = lax.broadcasted_iota(jnp.int32, (tq, tq), 1)
    top = step(i, kpos < qpos)

    def live(state):
        j, top = state
        return jnp.logical_and(j >= 0, top > -SB_UNDERFLOW_LOG2)

    def body(state):
        j, _ = state
        return j - 1, step(j, None)

    lax.while_loop(live, body, (i - 1, top))
    for p in range(npairs):
        o_ref[:, p * LANES:(p + 1) * LANES] = acc_ref[p].astype(o_ref.dtype)


def _sb_attention(qkv, batch, seq):
    tq = SB_BLOCK
    nq = seq // tq
    width = SB_PAIRS_PER_STEP * LANES
    nsteps = D_MODEL // width
    return pl.pallas_call(
        _sb_kernel,
        grid=(batch, nsteps, nq),
        in_specs=[pl.BlockSpec((tq, width), lambda b, p, i: (b * nq + i, p)),
                  pl.BlockSpec((seq, width), lambda b, p, i: (b, nsteps + p)),
                  pl.BlockSpec((seq, width), lambda b, p, i: (b, 2 * nsteps + p))],
        out_specs=pl.BlockSpec((tq, width), lambda b, p, i: (b * nq + i, p)),
        out_shape=jax.ShapeDtypeStruct((batch * seq, D_MODEL), BF16),
        scratch_shapes=[pltpu.VMEM((2 * SB_PAIRS_PER_STEP, tq, LANES), F32),
                        pltpu.VMEM((SB_PAIRS_PER_STEP, tq, LANES), F32)],
        compiler_params=_params("arbitrary", "arbitrary", "arbitrary"),
        name="sb_attention",
    )(qkv, qkv, qkv)


def kernel(x, mix_norm, ffn_norm, final_norm, ssd_w_in, ssd_conv_w, ssd_conv_b, ssd_dt_bias, ssd_a_log, ssd_d, ssd_norm, ssd_w_out, sb_w_qkv, sb_w_out, ffn_w_in, ffn_conv_w, ffn_conv_b, ffn_w_out):
    batch, seq, d = x.shape
    xf = x.reshape(batch * seq, d)
    h = _rmsnorm(xf, mix_norm[0])
    out = None
    for i in range(DEPTH):
        j = i // 2
        if i % 2 == 0:
            w = ssd_w_in[j]
            w_dt = jnp.pad(w[:, SSD_D_INNER + SSD_CONV_DIM:], ((0, 0), (0, LANES - SSD_HEADS)))
            z = _mm(h, w[:, :SSD_D_INNER].astype(BF16), BF16, 1024, 1024, "ssd_z")
            xbc = _mm_conv_silu(h, w[:, SSD_D_INNER:SSD_D_INNER + SSD_CONV_DIM].astype(BF16),
                                ssd_conv_w[j], ssd_conv_b[j], seq, 512, 1024, "ssd_xbc")
            dt_raw = _mm(h, w_dt.astype(BF16), F32, 1024, LANES, "ssd_dt")
            mixed = _ssd_scan(z, xbc, dt_raw, ssd_dt_bias[j], ssd_a_log[j], ssd_d[j], ssd_norm[j], batch, seq)
            w_out = ssd_w_out[j]
        else:
            q_scale = jnp.concatenate([jnp.full((d,), math.log2(math.e) * SB_HEAD_DIM ** -0.5, F32),
                                       jnp.ones((2 * d,), F32)])
            qkv = _mm(h, (sb_w_qkv[j] * q_scale).astype(BF16), BF16, 1024, 1024, "sb_qkv")
            mixed = _sb_attention(qkv, batch, seq)
            w_out = sb_w_out[j]
        xf, h = _mm_res_norm(mixed, w_out.astype(BF16), xf, ffn_norm[i], False, 512, "mixer_out")
        act = _mm_conv_glu(h, ffn_w_in[i].astype(BF16), ffn_conv_w[i], ffn_conv_b[i], seq, 512, FFN_D_FF // 2,
                           "ffn_in")
        if i + 1 < DEPTH:
            xf, h = _mm_res_norm(act, ffn_w_out[i].astype(BF16), xf, mix_norm[i + 1], False, 512, "ffn_out")
        else:
            out = _mm_res_norm(act, ffn_w_out[i].astype(BF16), xf, final_norm, True, 512, "ffn_out_final")
    return out.reshape(batch, seq, d)
```

```python
import functools
import math

import jax
import jax.numpy as jnp
from jax import lax
from jax.experimental import pallas as pl
from jax.experimental.pallas import tpu as pltpu

F32 = jnp.float32
BF16 = jnp.bfloat16

D_MODEL = 1024
DEPTH = 4
NORM_EPS = 1e-6
LOG2E = math.log2(math.e)

SSD_D_INNER = 2048
SSD_HEAD_DIM = 64
SSD_HEADS = 32
SSD_GROUPS = 8
SSD_HEADS_PER_GROUP = 4
SSD_STATE = 128
SSD_CHUNK = 128
SSD_CONV_DIM = SSD_D_INNER + 2 * SSD_GROUPS * SSD_STATE
SSD_GROUP_WIDTH = SSD_HEADS_PER_GROUP * SSD_HEAD_DIM

SB_HEADS = 16
SB_HEAD_DIM = 64
SB_BLOCK = 256
SB_PAIRS_PER_STEP = 2
SB_UNDERFLOW_LOG2 = 160.0

FFN_D_FF = 2816

LANES = 128
SUBLANES = 8
VMEM_LIMIT_BYTES = 56 * 1024 * 1024


def _params(*semantics):
    return pltpu.CompilerParams(dimension_semantics=semantics, vmem_limit_bytes=VMEM_LIMIT_BYTES)


def _rms_scale(v):
    return lax.rsqrt(jnp.mean(v * v, axis=-1, keepdims=True) + NORM_EPS)


def _rmsnorm_kernel(x_ref, g_ref, o_ref):
    x = x_ref[...]
    o_ref[...] = (x * _rms_scale(x) * g_ref[...]).astype(o_ref.dtype)


def _rmsnorm(x, g, tm=1024):
    t, d = x.shape
    return pl.pallas_call(
        _rmsnorm_kernel,
        grid=(t // tm,),
        in_specs=[pl.BlockSpec((tm, d), lambda i: (i, 0)), pl.BlockSpec((1, d), lambda i: (0, 0))],
        out_specs=pl.BlockSpec((tm, d), lambda i: (i, 0)),
        out_shape=jax.ShapeDtypeStruct((t, d), BF16),
        compiler_params=_params("arbitrary"),
        name="rmsnorm",
    )(x, g.reshape(1, d))


def _mm_kernel(h_ref, w_ref, o_ref):
    o_ref[...] = jnp.dot(h_ref[...], w_ref[...], preferred_element_type=F32).astype(o_ref.dtype)


def _mm(h, w, out_dtype, tm, tn, name):
    t, k = h.shape
    n = w.shape[1]
    return pl.pallas_call(
        _mm_kernel,
        grid=(n // tn, t // tm),
        in_specs=[pl.BlockSpec((tm, k), lambda j, i: (i, 0)), pl.BlockSpec((k, tn), lambda j, i: (0, j))],
        out_specs=pl.BlockSpec((tm, tn), lambda j, i: (i, j)),
        out_shape=jax.ShapeDtypeStruct((t, n), out_dtype),
        compiler_params=_params("arbitrary", "arbitrary"),
        name=name,
    )(h, w)


def _shift_rows(u, prev, j):
    rolled = pltpu.roll(u, j, axis=0)
    prev_rolled = pltpu.roll(prev, j, axis=0)
    row = lax.broadcasted_iota(jnp.int32, prev.shape, 0)
    top = jnp.where(row < j, prev_rolled, rolled[:SUBLANES])
    return jnp.concatenate([top, rolled[SUBLANES:]], axis=0)


def _causal_conv(u, prev, cw, cb):
    width = cw.shape[0]
    y = cb + cw[width - 1:width, :] * u
    for j in range(1, width):
        y = y + cw[width - 1 - j:width - j, :] * _shift_rows(u, prev, j)
    return y


def _silu(y):
    return y * (1.0 / (1.0 + jnp.exp2(y * -LOG2E)))


def _mm_conv_act_kernel(*refs, nbranch, tiles_per_seq):
    h_ref = refs[0]
    w_refs = refs[1:1 + nbranch]
    cw_refs = refs[1 + nbranch:1 + 2 * nbranch]
    cb_refs = refs[1 + 2 * nbranch:1 + 3 * nbranch]
    o_ref = refs[1 + 3 * nbranch]
    prev_refs = refs[2 + 3 * nbranch:]
    tm = h_ref.shape[0]

    @pl.when(pl.program_id(1) % tiles_per_seq == 0)
    def _():
        for prev_ref in prev_refs:
            prev_ref[...] = jnp.zeros_like(prev_ref)

    h = h_ref[...]
    ys = []
    for b in range(nbranch):
        u = jnp.dot(h, w_refs[b][...], preferred_element_type=F32)
        ys.append(_causal_conv(u, prev_refs[b][...], cw_refs[b][...], cb_refs[b][...]))
        prev_refs[b][...] = u[tm - SUBLANES:, :]
    act = _silu(ys[0])
    if nbranch == 2:
        act = act * ys[1]
    o_ref[...] = act.astype(o_ref.dtype)


def _mm_conv_act(h, w, cw, cb, nbranch, seq, tm, tn, name):
    t, k = h.shape
    n_out = w.shape[1] // nbranch
    width = cw.shape[0]
    ncol = n_out // tn
    in_specs = [pl.BlockSpec((tm, k), lambda j, i: (i, 0))]
    in_specs += [pl.BlockSpec((k, tn), lambda j, i, b=b: (0, j + b * ncol)) for b in range(nbranch)]
    in_specs += [pl.BlockSpec((width, tn), lambda j, i, b=b: (0, j + b * ncol)) for b in range(nbranch)]
    in_specs += [pl.BlockSpec((1, tn), lambda j, i, b=b: (0, j + b * ncol)) for b in range(nbranch)]
    cb2 = cb.reshape(1, -1)
    return pl.pallas_call(
        functools.partial(_mm_conv_act_kernel, nbranch=nbranch, tiles_per_seq=seq // tm),
        grid=(ncol, t // tm),
        in_specs=in_specs,
        out_specs=pl.BlockSpec((tm, tn), lambda j, i: (i, j)),
        out_shape=jax.ShapeDtypeStruct((t, n_out), BF16),
        scratch_shapes=[pltpu.VMEM((SUBLANES, tn), F32)] * nbranch,
        compiler_params=_params("arbitrary", "arbitrary"),
        name=name,
    )(h, *([w] * nbranch), *([cw] * nbranch), *([cb2] * nbranch))


def _mm_res_norm_kernel(a_ref, w_ref, x_ref, g_ref, xo_ref, ho_ref):
    xn = x_ref[...] + jnp.dot(a_ref[...], w_ref[...], preferred_element_type=F32)
    xo_ref[...] = xn
    ho_ref[...] = (xn * _rms_scale(xn) * g_ref[...]).astype(ho_ref.dtype)


def _mm_res_final_kernel(a_ref, w_ref, x_ref, g_ref, o_ref):
    xn = x_ref[...] + jnp.dot(a_ref[...], w_ref[...], preferred_element_type=F32)
    o_ref[...] = xn * _rms_scale(xn) * g_ref[...]


def _mm_res_norm(a, w, x, g, final, tm, name):
    t, k = a.shape
    d = w.shape[1]
    row = pl.BlockSpec((tm, d), lambda i: (i, 0))
    in_specs = [pl.BlockSpec((tm, k), lambda i: (i, 0)), pl.BlockSpec((k, d), lambda i: (0, 0)),
                row, pl.BlockSpec((1, d), lambda i: (0, 0))]
    if final:
        return pl.pallas_call(
            _mm_res_final_kernel, grid=(t // tm,), in_specs=in_specs, out_specs=row,
            out_shape=jax.ShapeDtypeStruct((t, d), F32),
            compiler_params=_params("arbitrary"), name=name,
        )(a, w, x, g.reshape(1, d))
    return pl.pallas_call(
        _mm_res_norm_kernel, grid=(t // tm,), in_specs=in_specs, out_specs=[row, row],
        out_shape=[jax.ShapeDtypeStruct((t, d), F32), jax.ShapeDtypeStruct((t, d), BF16)],
        compiler_params=_params("arbitrary"), name=name,
    )(a, w, x, g.reshape(1, d))


def _split3(v):
    a = v.astype(BF16)
    r = v - a.astype(F32)
    b = r.astype(BF16)
    c = (r - b.astype(F32)).astype(BF16)
    return a, b, c


def _expand_heads(v, h0, nheads):
    rows = v.shape[0]
    lane = lax.broadcasted_iota(jnp.int32, (rows, LANES), 1)
    pairs = []
    for p in range(nheads // 2):
        lo = jnp.broadcast_to(v[:, h0 + 2 * p:h0 + 2 * p + 1], (rows, LANES))
        hi = jnp.broadcast_to(v[:, h0 + 2 * p + 1:h0 + 2 * p + 2], (rows, LANES))
        pairs.append(jnp.where(lane < SSD_HEAD_DIM, lo, hi))
    return jnp.concatenate(pairs, axis=1)


def _ssd_kernel(z_ref, xbc_ref, dt_ref, dtb_ref, alog_ref, dskip_ref, ng_ref, o_ref, state_ref, y_ref):
    L = SSD_CHUNK

    @pl.when(pl.program_id(1) == 0)
    def _():
        state_ref[...] = jnp.zeros_like(state_ref)

    pre = dt_ref[...] + dtb_ref[...]
    dt = jnp.maximum(pre, 0.0) + jnp.log1p(jnp.exp(-jnp.abs(pre)))
    a = (-LOG2E * jnp.exp(alog_ref[...])) * dt
    row = lax.broadcasted_iota(jnp.int32, (L, L), 0)
    col = lax.broadcasted_iota(jnp.int32, (L, L), 1)
    causal = row >= col
    tri = jnp.where(causal, 1.0, 0.0).astype(BF16)
    a1, a2, a3 = _split3(a)
    acum = (jnp.dot(tri, a1, preferred_element_type=F32) + jnp.dot(tri, a2, preferred_element_type=F32)
            + jnp.dot(tri, a3, preferred_element_type=F32))
    acum_t = acum.T
    dt_t = dt.T
    last = acum[L - 1:L, :]
    exp_acum = jnp.exp2(acum)
    w_dt = jnp.exp2(last - acum) * dt
    exp_last = jnp.exp2(last)
    lane = lax.broadcasted_iota(jnp.int32, (L, LANES), 1)
    lo_lanes = lane < SSD_HEAD_DIM

    for g in range(SSD_GROUPS):
        b_g = xbc_ref[:, SSD_D_INNER + g * SSD_STATE:SSD_D_INNER + (g + 1) * SSD_STATE]
        c_off = SSD_D_INNER + SSD_GROUPS * SSD_STATE
        c_g = xbc_ref[:, c_off + g * SSD_STATE:c_off + (g + 1) * SSD_STATE]
        cb = lax.dot_general(c_g, b_g, (((1,), (1,)), ((), ())), preferred_element_type=F32)
        x0 = g * SSD_GROUP_WIDTH
        h0 = g * SSD_HEADS_PER_GROUP
        y_diag = []
        for p in range(SSD_HEADS_PER_GROUP // 2):
            scores = []
            for hh in (h0 + 2 * p, h0 + 2 * p + 1):
                seg = acum[:, hh:hh + 1] - acum_t[hh:hh + 1, :]
                decay = jnp.exp2(jnp.where(causal, seg, -jnp.inf))
                scores.append((cb * decay * dt_t[hh:hh + 1, :]).astype(BF16))
            xs = xbc_ref[:, x0 + p * LANES:x0 + (p + 1) * LANES]
            zero = jnp.zeros_like(xs)
            xs_split = jnp.concatenate([jnp.where(lo_lanes, xs, zero), jnp.where(lo_lanes, zero, xs)], axis=0)
            y_diag.append(jnp.dot(jnp.concatenate(scores, axis=1), xs_split, preferred_element_type=F32))
        y_diag = jnp.concatenate(y_diag, axis=1)

        xs_g = xbc_ref[:, x0:x0 + SSD_GROUP_WIDTH].astype(F32)
        state = state_ref[g]
        y_off = (jnp.dot(c_g, state.astype(BF16), preferred_element_type=F32)
                 * _expand_heads(exp_acum, h0, SSD_HEADS_PER_GROUP))
        xw = (xs_g * _expand_heads(w_dt, h0, SSD_HEADS_PER_GROUP)).astype(BF16)
        state_ref[g] = (state * _expand_heads(exp_last, h0, SSD_HEADS_PER_GROUP)
                        + lax.dot_general(b_g, xw, (((0,), (0,)), ((), ())), preferred_element_type=F32))
        y_ref[:, x0:x0 + SSD_GROUP_WIDTH] = y_diag + y_off + dskip_ref[:, x0:x0 + SSD_GROUP_WIDTH] * xs_g

    z = z_ref[...].astype(F32)
    gated = y_ref[...] * _silu(z)
    o_ref[...] = (gated * _rms_scale(gated) * ng_ref[...]).astype(o_ref.dtype)


def _ssd_scan(z, xbc, dt_raw, dt_bias, a_log, d_skip, norm_g, batch, seq):
    nc = seq // SSD_CHUNK
    L = SSD_CHUNK
    pad = LANES - SSD_HEADS
    dtb = jnp.pad(dt_bias, (0, pad)).reshape(1, LANES)
    alog = jnp.pad(a_log, (0, pad)).reshape(1, LANES)
    dskip = jnp.repeat(d_skip, SSD_HEAD_DIM).reshape(1, SSD_D_INNER)
    small = lambda n: pl.BlockSpec((1, n), lambda b, c: (0, 0))
    return pl.pallas_call(
        _ssd_kernel,
        grid=(batch, nc),
        in_specs=[pl.BlockSpec((L, SSD_D_INNER), lambda b, c: (b * nc + c, 0)),
                  pl.BlockSpec((L, SSD_CONV_DIM), lambda b, c: (b * nc + c, 0)),
                  pl.BlockSpec((L, LANES), lambda b, c: (b * nc + c, 0)),
                  small(LANES), small(LANES), small(SSD_D_INNER), small(SSD_D_INNER)],
        out_specs=pl.BlockSpec((L, SSD_D_INNER), lambda b, c: (b * nc + c, 0)),
        out_shape=jax.ShapeDtypeStruct((batch * seq, SSD_D_INNER), BF16),
        scratch_shapes=[pltpu.VMEM((SSD_GROUPS, SSD_STATE, SSD_GROUP_WIDTH), F32),
                        pltpu.VMEM((L, SSD_D_INNER), F32)],
        compiler_params=_params("arbitrary", "arbitrary"),
        name="ssd_scan",
    )(z, xbc, dt_raw, dtb, alog, dskip, norm_g.reshape(1, SSD_D_INNER))


def _sb_head_block(qh, kb, carry, suffix_mat, strict):
    tk = kb.shape[0]
    l = lax.dot_general(qh, kb, (((1,), (1,)), ((), ())), preferred_element_type=F32)
    neg_abs = lax.bitcast_convert_type(lax.bitcast_convert_type(l, jnp.uint32) | jnp.uint32(0x80000000), F32)
    lg = jnp.log2(1.0 + jnp.exp2(neg_abs))
    log_beta = jnp.minimum(l, 0.0) - lg
    log_fail = log_beta - l
    if strict is not None:
        log_fail = jnp.where(strict, log_fail, 0.0)
    inner = jnp.dot(log_fail.astype(BF16), suffix_mat, preferred_element_type=F32)
    w = jnp.exp2(log_beta + inner + jnp.concatenate([carry] * (tk // LANES), axis=1))
    if strict is not None:
        w = jnp.where(strict, w, 0.0)
    return w.astype(BF16), carry + jnp.sum(log_fail, axis=-1, keepdims=True)


def _sb_kernel(q_ref, k_ref, v_ref, o_ref, carry_ref, acc_ref):
    tq = SB_BLOCK
    i = pl.program_id(2)
    npairs = q_ref.shape[1] // LANES
    suffix_mat = jnp.where(lax.broadcasted_iota(jnp.int32, (tq, tq), 0) > lax.broadcasted_iota(jnp.int32, (tq, tq), 1),
                           1.0, 0.0).astype(BF16)
    lo_lanes = lax.broadcasted_iota(jnp.int32, (tq, LANES), 1) < SB_HEAD_DIM

    def step(j, strict):
        start = pl.multiple_of(j * tq, tq)
        top = None
        for p in range(npairs):
            lanes = slice(p * LANES, (p + 1) * LANES)
            q = q_ref[:, lanes]
            kb = k_ref[pl.ds(start, tq), lanes]
            vb = v_ref[pl.ds(start, tq), lanes]
            zero = jnp.zeros_like(q)
            v_split = jnp.concatenate([jnp.where(lo_lanes, vb, zero), jnp.where(lo_lanes, zero, vb)], axis=0)
            ws = []
            for hh, qh in enumerate((jnp.where(lo_lanes, q, zero), jnp.where(lo_lanes, zero, q))):
                w, carry = _sb_head_block(qh, kb, carry_ref[2 * p + hh], suffix_mat, strict)
                carry_ref[2 * p + hh] = carry
                ws.append(w)
                top = carry if top is None else jnp.maximum(top, carry)
            acc_ref[p] += jnp.dot(jnp.concatenate(ws, axis=1), v_split, preferred_element_type=F32)
        return jnp.max(top)

    carry_ref[...] = jnp.zeros_like(carry_ref)
    acc_ref[...] = jnp.zeros_like(acc_ref)
    qpos = lax.broadcasted_iota(jnp.int32, (tq, tq), 0)
    kpos = lax.broadcasted_iota(jnp.int32, (tq, tq), 1)
    top = step(i, kpos < qpos)

    def live(state):
        j, top = state
        return jnp.logical_and(j >= 0, top > -SB_UNDERFLOW_LOG2)

    def body(state):
        j, _ = state
        return j - 1, step(j, None)

    lax.while_loop(live, body, (i - 1, top))
    for p in range(npairs):
        o_ref[:, p * LANES:(p + 1) * LANES] = acc_ref[p].astype(o_ref.dtype)


def _sb_attention(qkv, batch, seq):
    tq = SB_BLOCK
    nq = seq // tq
    width = SB_PAIRS_PER_STEP * LANES
    nsteps = D_MODEL // width
    return pl.pallas_call(
        _sb_kernel,
        grid=(batch, nsteps, nq),
        in_specs=[pl.BlockSpec((tq, width), lambda b, p, i: (b * nq + i, p)),
                  pl.BlockSpec((seq, width), lambda b, p, i: (b, nsteps + p)),
                  pl.BlockSpec((seq, width), lambda b, p, i: (b, 2 * nsteps + p))],
        out_specs=pl.BlockSpec((tq, width), lambda b, p, i: (b * nq + i, p)),
        out_shape=jax.ShapeDtypeStruct((batch * seq, D_MODEL), BF16),
        scratch_shapes=[pltpu.VMEM((2 * SB_PAIRS_PER_STEP, tq, LANES), F32),
                        pltpu.VMEM((SB_PAIRS_PER_STEP, tq, LANES), F32)],
        compiler_params=_params("arbitrary", "arbitrary", "arbitrary"),
        name="sb_attention",
    )(qkv, qkv, qkv)


def kernel(x, mix_norm, ffn_norm, final_norm, ssd_w_in, ssd_conv_w, ssd_conv_b, ssd_dt_bias, ssd_a_log, ssd_d, ssd_norm, ssd_w_out, sb_w_qkv, sb_w_out, ffn_w_in, ffn_conv_w, ffn_conv_b, ffn_w_out):
    batch, seq, d = x.shape
    xf = x.reshape(batch * seq, d)
    h = _rmsnorm(xf, mix_norm[0])
    out = None
    for i in range(DEPTH):
        j = i // 2
        if i % 2 == 0:
            w = ssd_w_in[j]
            w_dt = jnp.pad(w[:, SSD_D_INNER + SSD_CONV_DIM:], ((0, 0), (0, LANES - SSD_HEADS)))
            z = _mm(h, w[:, :SSD_D_INNER].astype(BF16), BF16, 1024, 1024, "ssd_z")
            xbc = _mm_conv_act(h, w[:, SSD_D_INNER:SSD_D_INNER + SSD_CONV_DIM].astype(BF16),
                               ssd_conv_w[j], ssd_conv_b[j], 1, seq, 512, 1024, "ssd_xbc")
            dt_raw = _mm(h, w_dt.astype(BF16), F32, 1024, LANES, "ssd_dt")
            mixed = _ssd_scan(z, xbc, dt_raw, ssd_dt_bias[j], ssd_a_log[j], ssd_d[j], ssd_norm[j], batch, seq)
            w_out = ssd_w_out[j]
        else:
            q_scale = jnp.concatenate([jnp.full((d,), math.log2(math.e) * SB_HEAD_DIM ** -0.5, F32),
                                       jnp.ones((2 * d,), F32)])
            qkv = _mm(h, (sb_w_qkv[j] * q_scale).astype(BF16), BF16, 1024, 1024, "sb_qkv")
            mixed = _sb_attention(qkv, batch, seq)
            w_out = sb_w_out[j]
        xf, h = _mm_res_norm(mixed, w_out.astype(BF16), xf, ffn_norm[i], False, 512, "mixer_out")
        act = _mm_conv_act(h, ffn_w_in[i].astype(BF16), ffn_conv_w[i], ffn_conv_b[i], 2, seq, 512, FFN_D_FF // 2,
                           "ffn_in")
        if i + 1 < DEPTH:
            xf, h = _mm_res_norm(act, ffn_w_out[i].astype(BF16), xf, mix_norm[i + 1], False, 512, "ffn_out")
        else:
            out = _mm_res_norm(act, ffn_w_out[i].astype(BF16), xf, final_norm, True, 512, "ffn_out_final")
    return out.reshape(batch, seq, d)
```

```python
import functools
import math

import jax
import jax.numpy as jnp
from jax import lax
from jax.experimental import pallas as pl
from jax.experimental.pallas import tpu as pltpu

F32 = jnp.float32
BF16 = jnp.bfloat16

D_MODEL = 1024
DEPTH = 4
NORM_EPS = 1e-6
LOG2E = math.log2(math.e)

SSD_D_INNER = 2048
SSD_HEAD_DIM = 64
SSD_HEADS = 32
SSD_GROUPS = 8
SSD_HEADS_PER_GROUP = 4
SSD_STATE = 128
SSD_CHUNK = 128
SSD_CONV_DIM = SSD_D_INNER + 2 * SSD_GROUPS * SSD_STATE
SSD_GROUP_WIDTH = SSD_HEADS_PER_GROUP * SSD_HEAD_DIM

SB_HEADS = 16
SB_HEAD_DIM = 64
SB_BLOCK = 256
SB_PAIRS_PER_STEP = 4
SB_UNDERFLOW_LOG2 = 160.0

FFN_D_FF = 2816

LANES = 128
SUBLANES = 8
VMEM_LIMIT_BYTES = 56 * 1024 * 1024


def _params(*semantics):
    return pltpu.CompilerParams(dimension_semantics=semantics, vmem_limit_bytes=VMEM_LIMIT_BYTES)


def _rms_scale(v):
    return lax.rsqrt(jnp.mean(v * v, axis=-1, keepdims=True) + NORM_EPS)


def _rmsnorm_kernel(x_ref, g_ref, o_ref):
    x = x_ref[...]
    o_ref[...] = (x * _rms_scale(x) * g_ref[...]).astype(o_ref.dtype)


def _rmsnorm(x, g, tm=1024):
    t, d = x.shape
    return pl.pallas_call(
        _rmsnorm_kernel,
        grid=(t // tm,),
        in_specs=[pl.BlockSpec((tm, d), lambda i: (i, 0)), pl.BlockSpec((1, d), lambda i: (0, 0))],
        out_specs=pl.BlockSpec((tm, d), lambda i: (i, 0)),
        out_shape=jax.ShapeDtypeStruct((t, d), BF16),
        compiler_params=_params("arbitrary"),
        name="rmsnorm",
    )(x, g.reshape(1, d))


def _mm_kernel(h_ref, w_ref, o_ref, wb_ref, *, scaled_tiles, scale):
    @pl.when(pl.program_id(1) == 0)
    def _():
        w = w_ref[...]
        if scaled_tiles:
            w = w * jnp.where(pl.program_id(0) < scaled_tiles, scale, 1.0)
        wb_ref[...] = w.astype(BF16)

    o_ref[...] = jnp.dot(h_ref[...], wb_ref[...], preferred_element_type=F32).astype(o_ref.dtype)


def _mm(h, w, layer, col0, n, out_dtype, tm, tn, name, scaled_tiles=0, scale=1.0):
    t, k = h.shape
    off = col0 // tn
    return pl.pallas_call(
        functools.partial(_mm_kernel, scaled_tiles=scaled_tiles, scale=scale),
        grid=(n // tn, t // tm),
        in_specs=[pl.BlockSpec((tm, k), lambda j, i: (i, 0)),
                  pl.BlockSpec((None, k, tn), lambda j, i: (layer, 0, j + off), pipeline_mode=pl.Buffered(1))],
        out_specs=pl.BlockSpec((tm, tn), lambda j, i: (i, j)),
        out_shape=jax.ShapeDtypeStruct((t, n), out_dtype),
        scratch_shapes=[pltpu.VMEM((k, tn), BF16)],
        compiler_params=_params("arbitrary", "arbitrary"),
        name=name,
    )(h, w)


def _shift_rows(u, prev, j):
    rolled = pltpu.roll(u, j, axis=0)
    prev_rolled = pltpu.roll(prev, j, axis=0)
    row = lax.broadcasted_iota(jnp.int32, prev.shape, 0)
    top = jnp.where(row < j, prev_rolled, rolled[:SUBLANES])
    return jnp.concatenate([top, rolled[SUBLANES:]], axis=0)


def _causal_conv(u, prev, cw, cb):
    width = cw.shape[0]
    y = cb + cw[width - 1:width, :] * u
    for j in range(1, width):
        y = y + cw[width - 1 - j:width - j, :] * _shift_rows(u, prev, j)
    return y


def _silu(y):
    return y * (1.0 / (1.0 + jnp.exp2(y * -LOG2E)))


def _mm_conv_act_kernel(*refs, nbranch, tiles_per_seq):
    h_ref = refs[0]
    w_refs = refs[1:1 + nbranch]
    cw_refs = refs[1 + nbranch:1 + 2 * nbranch]
    cb_refs = refs[1 + 2 * nbranch:1 + 3 * nbranch]
    o_ref = refs[1 + 3 * nbranch]
    prev_refs = refs[2 + 3 * nbranch:2 + 4 * nbranch]
    wb_refs = refs[2 + 4 * nbranch:]
    tm = h_ref.shape[0]

    @pl.when(pl.program_id(1) == 0)
    def _():
        for w_ref, wb_ref in zip(w_refs, wb_refs):
            wb_ref[...] = w_ref[...].astype(BF16)

    @pl.when(pl.program_id(1) % tiles_per_seq == 0)
    def _():
        for prev_ref in prev_refs:
            prev_ref[...] = jnp.zeros_like(prev_ref)

    h = h_ref[...]
    ys = []
    for b in range(nbranch):
        u = jnp.dot(h, wb_refs[b][...], preferred_element_type=F32)
        ys.append(_causal_conv(u, prev_refs[b][...], cw_refs[b][...], cb_refs[b][...]))
        prev_refs[b][...] = u[tm - SUBLANES:, :]
    act = _silu(ys[0])
    if nbranch == 2:
        act = act * ys[1]
    o_ref[...] = act.astype(o_ref.dtype)


def _mm_conv_act(h, w, layer, col0, cw, cb, nbranch, seq, tm, tn, name):
    t, k = h.shape
    n_out = cw.shape[1] // nbranch
    width = cw.shape[0]
    ncol = n_out // tn
    off = col0 // tn
    in_specs = [pl.BlockSpec((tm, k), lambda j, i: (i, 0))]
    in_specs += [pl.BlockSpec((None, k, tn), lambda j, i, b=b: (layer, 0, j + b * ncol + off),
                              pipeline_mode=pl.Buffered(1)) for b in range(nbranch)]
    in_specs += [pl.BlockSpec((width, tn), lambda j, i, b=b: (0, j + b * ncol)) for b in range(nbranch)]
    in_specs += [pl.BlockSpec((1, tn), lambda j, i, b=b: (0, j + b * ncol)) for b in range(nbranch)]
    cb2 = cb.reshape(1, -1)
    return pl.pallas_call(
        functools.partial(_mm_conv_act_kernel, nbranch=nbranch, tiles_per_seq=seq // tm),
        grid=(ncol, t // tm),
        in_specs=in_specs,
        out_specs=pl.BlockSpec((tm, tn), lambda j, i: (i, j)),
        out_shape=jax.ShapeDtypeStruct((t, n_out), BF16),
        scratch_shapes=[pltpu.VMEM((SUBLANES, tn), F32)] * nbranch + [pltpu.VMEM((k, tn), BF16)] * nbranch,
        compiler_params=_params("arbitrary", "arbitrary"),
        name=name,
    )(h, *([w] * nbranch), *([cw] * nbranch), *([cb2] * nbranch))


def _residual_update(a_ref, w_ref, x_ref, wb_ref):
    @pl.when(pl.program_id(0) == 0)
    def _():
        wb_ref[...] = w_ref[...].astype(BF16)

    return x_ref[...] + jnp.dot(a_ref[...], wb_ref[...], preferred_element_type=F32)


def _mm_res_norm_kernel(a_ref, w_ref, x_ref, g_ref, xo_ref, ho_ref, wb_ref):
    xn = _residual_update(a_ref, w_ref, x_ref, wb_ref)
    xo_ref[...] = xn
    ho_ref[...] = (xn * _rms_scale(xn) * g_ref[...]).astype(ho_ref.dtype)


def _mm_res_final_kernel(a_ref, w_ref, x_ref, g_ref, o_ref, wb_ref):
    xn = _residual_update(a_ref, w_ref, x_ref, wb_ref)
    o_ref[...] = xn * _rms_scale(xn) * g_ref[...]


def _mm_res_norm(a, w, layer, x, g, final, tm, name):
    t, k = a.shape
    d = w.shape[2]
    row = pl.BlockSpec((tm, d), lambda i: (i, 0))
    in_specs = [pl.BlockSpec((tm, k), lambda i: (i, 0)),
                pl.BlockSpec((None, k, d), lambda i: (layer, 0, 0), pipeline_mode=pl.Buffered(1)),
                row, pl.BlockSpec((1, d), lambda i: (0, 0))]
    scratch = [pltpu.VMEM((k, d), BF16)]
    if final:
        return pl.pallas_call(
            _mm_res_final_kernel, grid=(t // tm,), in_specs=in_specs, out_specs=row,
            out_shape=jax.ShapeDtypeStruct((t, d), F32), scratch_shapes=scratch,
            compiler_params=_params("arbitrary"), name=name,
        )(a, w, x, g.reshape(1, d))
    return pl.pallas_call(
        _mm_res_norm_kernel, grid=(t // tm,), in_specs=in_specs, out_specs=[row, row],
        out_shape=[jax.ShapeDtypeStruct((t, d), F32), jax.ShapeDtypeStruct((t, d), BF16)], scratch_shapes=scratch,
        compiler_params=_params("arbitrary"), name=name,
    )(a, w, x, g.reshape(1, d))


def _split3(v):
    a = v.astype(BF16)
    r = v - a.astype(F32)
    b = r.astype(BF16)
    c = (r - b.astype(F32)).astype(BF16)
    return a, b, c


def _expand_heads(v, h0, nheads):
    rows = v.shape[0]
    lane = lax.broadcasted_iota(jnp.int32, (rows, LANES), 1)
    pairs = []
    for p in range(nheads // 2):
        lo = jnp.broadcast_to(v[:, h0 + 2 * p:h0 + 2 * p + 1], (rows, LANES))
        hi = jnp.broadcast_to(v[:, h0 + 2 * p + 1:h0 + 2 * p + 2], (rows, LANES))
        pairs.append(jnp.where(lane < SSD_HEAD_DIM, lo, hi))
    return jnp.concatenate(pairs, axis=1)


def _ssd_kernel(z_ref, xbc_ref, dt_ref, dtb_ref, alog_ref, dskip_ref, ng_ref, o_ref, state_ref, y_ref):
    L = SSD_CHUNK

    @pl.when(pl.program_id(1) == 0)
    def _():
        state_ref[...] = jnp.zeros_like(state_ref)

    pre = dt_ref[...] + dtb_ref[...]
    dt = jnp.maximum(pre, 0.0) + jnp.log1p(jnp.exp(-jnp.abs(pre)))
    a = (-LOG2E * jnp.exp(alog_ref[...])) * dt
    row = lax.broadcasted_iota(jnp.int32, (L, L), 0)
    col = lax.broadcasted_iota(jnp.int32, (L, L), 1)
    causal = row >= col
    tri = jnp.where(causal, 1.0, 0.0).astype(BF16)
    a1, a2, a3 = _split3(a)
    acum = (jnp.dot(tri, a1, preferred_element_type=F32) + jnp.dot(tri, a2, preferred_element_type=F32)
            + jnp.dot(tri, a3, preferred_element_type=F32))
    acum_t = acum.T
    dt_t = dt.T
    last = acum[L - 1:L, :]
    exp_acum = jnp.exp2(acum)
    w_dt = jnp.exp2(last - acum) * dt
    exp_last = jnp.exp2(last)
    lane = lax.broadcasted_iota(jnp.int32, (L, LANES), 1)
    lo_lanes = lane < SSD_HEAD_DIM

    for g in range(SSD_GROUPS):
        b_g = xbc_ref[:, SSD_D_INNER + g * SSD_STATE:SSD_D_INNER + (g + 1) * SSD_STATE]
        c_off = SSD_D_INNER + SSD_GROUPS * SSD_STATE
        c_g = xbc_ref[:, c_off + g * SSD_STATE:c_off + (g + 1) * SSD_STATE]
        cb = lax.dot_general(c_g, b_g, (((1,), (1,)), ((), ())), preferred_element_type=F32)
        x0 = g * SSD_GROUP_WIDTH
        h0 = g * SSD_HEADS_PER_GROUP
        y_diag = []
        for p in range(SSD_HEADS_PER_GROUP // 2):
            scores = []
            for hh in (h0 + 2 * p, h0 + 2 * p + 1):
                seg = acum[:, hh:hh + 1] - acum_t[hh:hh + 1, :]
                decay = jnp.exp2(jnp.where(causal, seg, -jnp.inf))
                scores.append((cb * decay * dt_t[hh:hh + 1, :]).astype(BF16))
            xs = xbc_ref[:, x0 + p * LANES:x0 + (p + 1) * LANES]
            zero = jnp.zeros_like(xs)
            xs_split = jnp.concatenate([jnp.where(lo_lanes, xs, zero), jnp.where(lo_lanes, zero, xs)], axis=0)
            y_diag.append(jnp.dot(jnp.concatenate(scores, axis=1), xs_split, preferred_element_type=F32))
        y_diag = jnp.concatenate(y_diag, axis=1)

        xs_g = xbc_ref[:, x0:x0 + SSD_GROUP_WIDTH].astype(F32)
        state = state_ref[g]
        y_off = (jnp.dot(c_g, state.astype(BF16), preferred_element_type=F32)
                 * _expand_heads(exp_acum, h0, SSD_HEADS_PER_GROUP))
        xw = (xs_g * _expand_heads(w_dt, h0, SSD_HEADS_PER_GROUP)).astype(BF16)
        state_ref[g] = (state * _expand_heads(exp_last, h0, SSD_HEADS_PER_GROUP)
                        + lax.dot_general(b_g, xw, (((0,), (0,)), ((), ())), preferred_element_type=F32))
        y_ref[:, x0:x0 + SSD_GROUP_WIDTH] = y_diag + y_off + dskip_ref[:, x0:x0 + SSD_GROUP_WIDTH] * xs_g

    z = z_ref[...].astype(F32)
    gated = y_ref[...] * _silu(z)
    o_ref[...] = (gated * _rms_scale(gated) * ng_ref[...]).astype(o_ref.dtype)


def _ssd_scan(z, xbc, dt_raw, dt_bias, a_log, d_skip, norm_g, batch, seq):
    nc = seq // SSD_CHUNK
    L = SSD_CHUNK
    pad = LANES - SSD_HEADS
    dtb = jnp.pad(dt_bias, (0, pad)).reshape(1, LANES)
    alog = jnp.pad(a_log, (0, pad)).reshape(1, LANES)
    dskip = jnp.repeat(d_skip, SSD_HEAD_DIM).reshape(1, SSD_D_INNER)
    small = lambda n: pl.BlockSpec((1, n), lambda b, c: (0, 0))
    return pl.pallas_call(
        _ssd_kernel,
        grid=(batch, nc),
        in_specs=[pl.BlockSpec((L, SSD_D_INNER), lambda b, c: (b * nc + c, 0)),
                  pl.BlockSpec((L, SSD_CONV_DIM), lambda b, c: (b * nc + c, 0)),
                  pl.BlockSpec((L, LANES), lambda b, c: (b * nc + c, 0)),
                  small(LANES), small(LANES), small(SSD_D_INNER), small(SSD_D_INNER)],
        out_specs=pl.BlockSpec((L, SSD_D_INNER), lambda b, c: (b * nc + c, 0)),
        out_shape=jax.ShapeDtypeStruct((batch * seq, SSD_D_INNER), BF16),
        scratch_shapes=[pltpu.VMEM((SSD_GROUPS, SSD_STATE, SSD_GROUP_WIDTH), F32),
                        pltpu.VMEM((L, SSD_D_INNER), F32)],
        compiler_params=_params("arbitrary", "arbitrary"),
        name="ssd_scan",
    )(z, xbc, dt_raw, dtb, alog, dskip, norm_g.reshape(1, SSD_D_INNER))


def _sb_head_block(qh, kb, carry, suffix_mat, strict):
    tk = kb.shape[0]
    l = lax.dot_general(qh, kb, (((1,), (1,)), ((), ())), preferred_element_type=F32)
    neg_abs = lax.bitcast_convert_type(lax.bitcast_convert_type(l, jnp.uint32) | jnp.uint32(0x80000000), F32)
    lg = jnp.log2(1.0 + jnp.exp2(neg_abs))
    log_beta = jnp.minimum(l, 0.0) - lg
    log_fail = log_beta - l
    if strict is not None:
        log_fail = jnp.where(strict, log_fail, 0.0)
    inner = jnp.dot(log_fail.astype(BF16), suffix_mat, preferred_element_type=F32)
    w = jnp.exp2(log_beta + inner + jnp.concatenate([carry] * (tk // LANES), axis=1))
    if strict is not None:
        w = jnp.where(strict, w, 0.0)
    return w.astype(BF16), carry + jnp.sum(log_fail, axis=-1, keepdims=True)


def _sb_kernel(q_ref, k_ref, v_ref, o_ref, carry_ref, acc_ref):
    tq = SB_BLOCK
    i = pl.program_id(2)
    npairs = q_ref.shape[1] // LANES
    suffix_mat = jnp.where(lax.broadcasted_iota(jnp.int32, (tq, tq), 0) > lax.broadcasted_iota(jnp.int32, (tq, tq), 1),
                           1.0, 0.0).astype(BF16)
    lo_lanes = lax.broadcasted_iota(jnp.int32, (tq, LANES), 1) < SB_HEAD_DIM

    def step(j, strict):
        start = pl.multiple_of(j * tq, tq)
        top = None
        for p in range(npairs):
            lanes = slice(p * LANES, (p + 1) * LANES)
            q = q_ref[:, lanes]
            kb = k_ref[pl.ds(start, tq), lanes]
            vb = v_ref[pl.ds(start, tq), lanes]
            zero = jnp.zeros_like(q)
            v_split = jnp.concatenate([jnp.where(lo_lanes, vb, zero), jnp.where(lo_lanes, zero, vb)], axis=0)
            ws = []
            for hh, qh in enumerate((jnp.where(lo_lanes, q, zero), jnp.where(lo_lanes, zero, q))):
                w, carry = _sb_head_block(qh, kb, carry_ref[2 * p + hh], suffix_mat, strict)
                carry_ref[2 * p + hh] = carry
                ws.append(w)
                top = carry if top is None else jnp.maximum(top, carry)
            acc_ref[p] += jnp.dot(jnp.concatenate(ws, axis=1), v_split, preferred_element_type=F32)
        return jnp.max(top)

    carry_ref[...] = jnp.zeros_like(carry_ref)
    acc_ref[...] = jnp.zeros_like(acc_ref)
    qpos = lax.broadcasted_iota(jnp.int32, (tq, tq), 0)
    kpos = lax.broadcasted_iota(jnp.int32, (tq, tq), 1)
    top = step(i, kpos < qpos)

    def live(state):
        j, top = state
        return jnp.logical_and(j >= 0, top > -SB_UNDERFLOW_LOG2)

    def body(state):
        j, _ = state
        return j - 1, step(j, None)

    lax.while_loop(live, body, (i - 1, top))
    for p in range(npairs):
        o_ref[:, p * LANES:(p + 1) * LANES] = acc_ref[p].astype(o_ref.dtype)


def _sb_attention(qkv, batch, seq):
    tq = SB_BLOCK
    nq = seq // tq
    width = SB_PAIRS_PER_STEP * LANES
    nsteps = D_MODEL // width
    return pl.pallas_call(
        _sb_kernel,
        grid=(batch, nsteps, nq),
        in_specs=[pl.BlockSpec((tq, width), lambda b, p, i: (b * nq + i, p)),
                  pl.BlockSpec((seq, width), lambda b, p, i: (b, nsteps + p)),
                  pl.BlockSpec((seq, width), lambda b, p, i: (b, 2 * nsteps + p))],
        out_specs=pl.BlockSpec((tq, width), lambda b, p, i: (b * nq + i, p)),
        out_shape=jax.ShapeDtypeStruct((batch * seq, D_MODEL), BF16),
        scratch_shapes=[pltpu.VMEM((2 * SB_PAIRS_PER_STEP, tq, LANES), F32),
                        pltpu.VMEM((SB_PAIRS_PER_STEP, tq, LANES), F32)],
        compiler_params=_params("arbitrary", "arbitrary", "arbitrary"),
        name="sb_attention",
    )(qkv, qkv, qkv)


def kernel(x, mix_norm, ffn_norm, final_norm, ssd_w_in, ssd_conv_w, ssd_conv_b, ssd_dt_bias, ssd_a_log, ssd_d, ssd_norm, ssd_w_out, sb_w_qkv, sb_w_out, ffn_w_in, ffn_conv_w, ffn_conv_b, ffn_w_out):
    batch, seq, d = x.shape
    xf = x.reshape(batch * seq, d)
    h = _rmsnorm(xf, mix_norm[0])
    out = None
    for i in range(DEPTH):
        j = i // 2
        if i % 2 == 0:
            w_dt = jnp.pad(ssd_w_in[j, :, SSD_D_INNER + SSD_CONV_DIM:], ((0, 0), (0, LANES - SSD_HEADS)))[None]
            z = _mm(h, ssd_w_in, j, 0, SSD_D_INNER, BF16, 1024, 1024, "ssd_z")
            xbc = _mm_conv_act(h, ssd_w_in, j, SSD_D_INNER, ssd_conv_w[j], ssd_conv_b[j], 1, seq, 1024, 1024, "ssd_xbc")
            dt_raw = _mm(h, w_dt, 0, 0, LANES, F32, 1024, LANES, "ssd_dt")
            mixed = _ssd_scan(z, xbc, dt_raw, ssd_dt_bias[j], ssd_a_log[j], ssd_d[j], ssd_norm[j], batch, seq)
            w_out = ssd_w_out
        else:
            qkv = _mm(h, sb_w_qkv, j, 0, 3 * d, BF16, 1024, 1024, "sb_qkv",
                      scaled_tiles=d // 1024, scale=LOG2E * SB_HEAD_DIM ** -0.5)
            mixed = _sb_attention(qkv, batch, seq)
            w_out = sb_w_out
        xf, h = _mm_res_norm(mixed, w_out, j, xf, ffn_norm[i], False, 512, "mixer_out")
        act = _mm_conv_act(h, ffn_w_in, i, 0, ffn_conv_w[i], ffn_conv_b[i], 2, seq, 512, FFN_D_FF // 2, "ffn_in")
        if i + 1 < DEPTH:
            xf, h = _mm_res_norm(act, ffn_w_out, i, xf, mix_norm[i + 1], False, 512, "ffn_out")
        else:
            out = _mm_res_norm(act, ffn_w_out, i, xf, final_norm, True, 512, "ffn_out_final")
    return out.reshape(batch, seq, d)
```

```python
import functools
import math

import jax
import jax.numpy as jnp
from jax import lax
from jax.experimental import pallas as pl
from jax.experimental.pallas import tpu as pltpu

F32 = jnp.float32
BF16 = jnp.bfloat16

D_MODEL = 1024
DEPTH = 4
NORM_EPS = 1e-6
LOG2E = math.log2(math.e)

SSD_D_INNER = 2048
SSD_HEAD_DIM = 64
SSD_HEADS = 32
SSD_GROUPS = 8
SSD_HEADS_PER_GROUP = 4
SSD_STATE = 128
SSD_CHUNK = 128
SSD_CHUNKS_PER_STEP = 4
SSD_CONV_DIM = SSD_D_INNER + 2 * SSD_GROUPS * SSD_STATE
SSD_GROUP_WIDTH = SSD_HEADS_PER_GROUP * SSD_HEAD_DIM

SB_HEADS = 16
SB_HEAD_DIM = 64
SB_BLOCK = 256
SB_PAIRS_PER_STEP = 4
SB_UNDERFLOW_LOG2 = 160.0

FFN_D_FF = 2816

LANES = 128
SUBLANES = 8
VMEM_LIMIT_BYTES = 56 * 1024 * 1024


def _params(*semantics):
    return pltpu.CompilerParams(dimension_semantics=semantics, vmem_limit_bytes=VMEM_LIMIT_BYTES)


def _rms_scale(v):
    return lax.rsqrt(jnp.mean(v * v, axis=-1, keepdims=True) + NORM_EPS)


def _rmsnorm_kernel(x_ref, g_ref, o_ref):
    x = x_ref[...]
    o_ref[...] = (x * _rms_scale(x) * g_ref[...]).astype(o_ref.dtype)


def _rmsnorm(x, g, tm=1024):
    t, d = x.shape
    return pl.pallas_call(
        _rmsnorm_kernel,
        grid=(t // tm,),
        in_specs=[pl.BlockSpec((tm, d), lambda i: (i, 0)), pl.BlockSpec((1, d), lambda i: (0, 0))],
        out_specs=pl.BlockSpec((tm, d), lambda i: (i, 0)),
        out_shape=jax.ShapeDtypeStruct((t, d), BF16),
        compiler_params=_params("arbitrary"),
        name="rmsnorm",
    )(x, g.reshape(1, d))


def _mm_kernel(h_ref, w_ref, o_ref, wb_ref, *, scaled_tiles, scale):
    @pl.when(pl.program_id(1) == 0)
    def _():
        w = w_ref[...]
        if scaled_tiles:
            w = w * jnp.where(pl.program_id(0) < scaled_tiles, scale, 1.0)
        wb_ref[...] = w.astype(BF16)

    o_ref[...] = jnp.dot(h_ref[...], wb_ref[...], preferred_element_type=F32).astype(o_ref.dtype)


def _mm(h, w, layer, col0, n, out_dtype, tm, tn, name, scaled_tiles=0, scale=1.0):
    t, k = h.shape
    off = col0 // tn
    return pl.pallas_call(
        functools.partial(_mm_kernel, scaled_tiles=scaled_tiles, scale=scale),
        grid=(n // tn, t // tm),
        in_specs=[pl.BlockSpec((tm, k), lambda j, i: (i, 0)),
                  pl.BlockSpec((None, k, tn), lambda j, i: (layer, 0, j + off), pipeline_mode=pl.Buffered(1))],
        out_specs=pl.BlockSpec((tm, tn), lambda j, i: (i, j)),
        out_shape=jax.ShapeDtypeStruct((t, n), out_dtype),
        scratch_shapes=[pltpu.VMEM((k, tn), BF16)],
        compiler_params=_params("arbitrary", "arbitrary"),
        name=name,
    )(h, w)


def _shift_rows(u, prev, j):
    rolled = pltpu.roll(u, j, axis=0)
    prev_rolled = pltpu.roll(prev, j, axis=0)
    row = lax.broadcasted_iota(jnp.int32, prev.shape, 0)
    top = jnp.where(row < j, prev_rolled, rolled[:SUBLANES])
    return jnp.concatenate([top, rolled[SUBLANES:]], axis=0)


def _causal_conv(u, prev, cw, cb):
    width = cw.shape[0]
    y = cb + cw[width - 1:width, :] * u
    for j in range(1, width):
        y = y + cw[width - 1 - j:width - j, :] * _shift_rows(u, prev, j)
    return y


def _silu(y):
    return y * (1.0 / (1.0 + jnp.exp2(y * -LOG2E)))


def _mm_conv_act_kernel(*refs, nbranch, tiles_per_seq):
    h_ref = refs[0]
    w_refs = refs[1:1 + nbranch]
    cw_refs = refs[1 + nbranch:1 + 2 * nbranch]
    cb_refs = refs[1 + 2 * nbranch:1 + 3 * nbranch]
    o_ref = refs[1 + 3 * nbranch]
    prev_refs = refs[2 + 3 * nbranch:2 + 4 * nbranch]
    wb_refs = refs[2 + 4 * nbranch:]
    tm = h_ref.shape[0]

    @pl.when(pl.program_id(1) == 0)
    def _():
        for w_ref, wb_ref in zip(w_refs, wb_refs):
            wb_ref[...] = w_ref[...].astype(BF16)

    @pl.when(pl.program_id(1) % tiles_per_seq == 0)
    def _():
        for prev_ref in prev_refs:
            prev_ref[...] = jnp.zeros_like(prev_ref)

    h = h_ref[...]
    ys = []
    for b in range(nbranch):
        u = jnp.dot(h, wb_refs[b][...], preferred_element_type=F32)
        ys.append(_causal_conv(u, prev_refs[b][...], cw_refs[b][...], cb_refs[b][...]))
        prev_refs[b][...] = u[tm - SUBLANES:, :]
    act = _silu(ys[0])
    if nbranch == 2:
        act = act * ys[1]
    o_ref[...] = act.astype(o_ref.dtype)


def _mm_conv_act(h, w, layer, col0, cw, cb, nbranch, seq, tm, tn, name):
    t, k = h.shape
    n_out = cw.shape[1] // nbranch
    width = cw.shape[0]
    ncol = n_out // tn
    off = col0 // tn
    in_specs = [pl.BlockSpec((tm, k), lambda j, i: (i, 0))]
    in_specs += [pl.BlockSpec((None, k, tn), lambda j, i, b=b: (layer, 0, j + b * ncol + off),
                              pipeline_mode=pl.Buffered(1)) for b in range(nbranch)]
    in_specs += [pl.BlockSpec((width, tn), lambda j, i, b=b: (0, j + b * ncol)) for b in range(nbranch)]
    in_specs += [pl.BlockSpec((1, tn), lambda j, i, b=b: (0, j + b * ncol)) for b in range(nbranch)]
    cb2 = cb.reshape(1, -1)
    return pl.pallas_call(
        functools.partial(_mm_conv_act_kernel, nbranch=nbranch, tiles_per_seq=seq // tm),
        grid=(ncol, t // tm),
        in_specs=in_specs,
        out_specs=pl.BlockSpec((tm, tn), lambda j, i: (i, j)),
        out_shape=jax.ShapeDtypeStruct((t, n_out), BF16),
        scratch_shapes=[pltpu.VMEM((SUBLANES, tn), F32)] * nbranch + [pltpu.VMEM((k, tn), BF16)] * nbranch,
        compiler_params=_params("arbitrary", "arbitrary"),
        name=name,
    )(h, *([w] * nbranch), *([cw] * nbranch), *([cb2] * nbranch))


def _residual_update(a_ref, w_ref, x_ref, wb_ref):
    @pl.when(pl.program_id(0) == 0)
    def _():
        wb_ref[...] = w_ref[...].astype(BF16)

    return x_ref[...] + jnp.dot(a_ref[...], wb_ref[...], preferred_element_type=F32)


def _mm_res_norm_kernel(a_ref, w_ref, x_ref, g_ref, xo_ref, ho_ref, wb_ref):
    xn = _residual_update(a_ref, w_ref, x_ref, wb_ref)
    xo_ref[...] = xn
    ho_ref[...] = (xn * _rms_scale(xn) * g_ref[...]).astype(ho_ref.dtype)


def _mm_res_final_kernel(a_ref, w_ref, x_ref, g_ref, o_ref, wb_ref):
    xn = _residual_update(a_ref, w_ref, x_ref, wb_ref)
    o_ref[...] = xn * _rms_scale(xn) * g_ref[...]


def _mm_res_norm(a, w, layer, x, g, final, tm, name):
    t, k = a.shape
    d = w.shape[2]
    row = pl.BlockSpec((tm, d), lambda i: (i, 0))
    in_specs = [pl.BlockSpec((tm, k), lambda i: (i, 0)),
                pl.BlockSpec((None, k, d), lambda i: (layer, 0, 0), pipeline_mode=pl.Buffered(1)),
                row, pl.BlockSpec((1, d), lambda i: (0, 0))]
    scratch = [pltpu.VMEM((k, d), BF16)]
    if final:
        return pl.pallas_call(
            _mm_res_final_kernel, grid=(t // tm,), in_specs=in_specs, out_specs=row,
            out_shape=jax.ShapeDtypeStruct((t, d), F32), scratch_shapes=scratch,
            compiler_params=_params("arbitrary"), name=name,
        )(a, w, x, g.reshape(1, d))
    return pl.pallas_call(
        _mm_res_norm_kernel, grid=(t // tm,), in_specs=in_specs, out_specs=[row, row],
        out_shape=[jax.ShapeDtypeStruct((t, d), F32), jax.ShapeDtypeStruct((t, d), BF16)], scratch_shapes=scratch,
        compiler_params=_params("arbitrary"), name=name,
    )(a, w, x, g.reshape(1, d))


def _split3(v):
    a = v.astype(BF16)
    r = v - a.astype(F32)
    b = r.astype(BF16)
    c = (r - b.astype(F32)).astype(BF16)
    return a, b, c


def _expand_heads(v, h0, nheads):
    rows = v.shape[0]
    lane = lax.broadcasted_iota(jnp.int32, (rows, LANES), 1)
    pairs = []
    for p in range(nheads // 2):
        if rows % SUBLANES:
            lo = jnp.broadcast_to(v[:, h0 + 2 * p:h0 + 2 * p + 1], (rows, LANES))
            hi = jnp.broadcast_to(v[:, h0 + 2 * p + 1:h0 + 2 * p + 2], (rows, LANES))
            pairs.append(jnp.where(lane < SSD_HEAD_DIM, lo, hi))
        else:
            pairs.append(jnp.take_along_axis(v, h0 + 2 * p + lane // SSD_HEAD_DIM, axis=1))
    return jnp.concatenate(pairs, axis=1)


def _ssd_kernel(z_ref, xbc_ref, dt_ref, dtb_ref, alog_ref, dskip_ref, ng_ref, o_ref, state_ref, y_ref):
    @pl.when(pl.program_id(1) == 0)
    def _():
        state_ref[...] = jnp.zeros_like(state_ref)

    for c in range(SSD_CHUNKS_PER_STEP):
        _ssd_chunk(slice(c * SSD_CHUNK, (c + 1) * SSD_CHUNK), z_ref, xbc_ref, dt_ref, dtb_ref, alog_ref, dskip_ref,
                   ng_ref, o_ref, state_ref, y_ref)


def _ssd_chunk(rows, z_ref, xbc_ref, dt_ref, dtb_ref, alog_ref, dskip_ref, ng_ref, o_ref, state_ref, y_ref):
    L = SSD_CHUNK
    pre = dt_ref[rows, :] + dtb_ref[...]
    dt = jnp.maximum(pre, 0.0) + jnp.log1p(jnp.exp(-jnp.abs(pre)))
    a = (-LOG2E * jnp.exp(alog_ref[...])) * dt
    row = lax.broadcasted_iota(jnp.int32, (L, L), 0)
    col = lax.broadcasted_iota(jnp.int32, (L, L), 1)
    causal = row >= col
    tri = jnp.where(causal, 1.0, 0.0).astype(BF16)
    a1, a2, a3 = _split3(a)
    acum = (jnp.dot(tri, a1, preferred_element_type=F32) + jnp.dot(tri, a2, preferred_element_type=F32)
            + jnp.dot(tri, a3, preferred_element_type=F32))
    src_t = (acum - jnp.log2(jnp.maximum(dt, jnp.finfo(F32).tiny))).T
    last = acum[L - 1:L, :]
    exp_acum = jnp.exp2(acum)
    w_dt = jnp.exp2(last - acum) * dt
    exp_last = jnp.exp2(last)
    lane = lax.broadcasted_iota(jnp.int32, (L, LANES), 1)
    lo_lanes = lane < SSD_HEAD_DIM

    for g in range(SSD_GROUPS):
        b_g = xbc_ref[rows, SSD_D_INNER + g * SSD_STATE:SSD_D_INNER + (g + 1) * SSD_STATE]
        c_off = SSD_D_INNER + SSD_GROUPS * SSD_STATE
        c_g = xbc_ref[rows, c_off + g * SSD_STATE:c_off + (g + 1) * SSD_STATE]
        cb = lax.dot_general(c_g, b_g, (((1,), (1,)), ((), ())), preferred_element_type=F32)
        x0 = g * SSD_GROUP_WIDTH
        h0 = g * SSD_HEADS_PER_GROUP
        y_diag = []
        for p in range(SSD_HEADS_PER_GROUP // 2):
            scores = []
            for hh in (h0 + 2 * p, h0 + 2 * p + 1):
                seg = acum[:, hh:hh + 1] - src_t[hh:hh + 1, :]
                scores.append((cb * jnp.exp2(jnp.where(causal, seg, -jnp.inf))).astype(BF16))
            xs = xbc_ref[rows, x0 + p * LANES:x0 + (p + 1) * LANES]
            zero = jnp.zeros_like(xs)
            xs_split = jnp.concatenate([jnp.where(lo_lanes, xs, zero), jnp.where(lo_lanes, zero, xs)], axis=0)
            y_diag.append(jnp.dot(jnp.concatenate(scores, axis=1), xs_split, preferred_element_type=F32))
        y_diag = jnp.concatenate(y_diag, axis=1)

        xs_g = xbc_ref[rows, x0:x0 + SSD_GROUP_WIDTH].astype(F32)
        state = state_ref[g]
        y_off = (jnp.dot(c_g, state.astype(BF16), preferred_element_type=F32)
                 * _expand_heads(exp_acum, h0, SSD_HEADS_PER_GROUP))
        xw = (xs_g * _expand_heads(w_dt, h0, SSD_HEADS_PER_GROUP)).astype(BF16)
        state_ref[g] = (state * _expand_heads(exp_last, h0, SSD_HEADS_PER_GROUP)
                        + lax.dot_general(b_g, xw, (((0,), (0,)), ((), ())), preferred_element_type=F32))
        y_ref[rows, x0:x0 + SSD_GROUP_WIDTH] = y_diag + y_off + dskip_ref[:, x0:x0 + SSD_GROUP_WIDTH] * xs_g

    z = z_ref[rows, :].astype(F32)
    gated = y_ref[rows, :] * _silu(z)
    o_ref[rows, :] = (gated * _rms_scale(gated) * ng_ref[...]).astype(o_ref.dtype)


def _ssd_scan(z, xbc, dt_raw, dt_bias, a_log, d_skip, norm_g, batch, seq):
    L = SSD_CHUNKS_PER_STEP * SSD_CHUNK
    nc = seq // L
    pad = LANES - SSD_HEADS
    dtb = jnp.pad(dt_bias, (0, pad)).reshape(1, LANES)
    alog = jnp.pad(a_log, (0, pad)).reshape(1, LANES)
    dskip = jnp.repeat(d_skip, SSD_HEAD_DIM).reshape(1, SSD_D_INNER)
    small = lambda n: pl.BlockSpec((1, n), lambda b, c: (0, 0))
    return pl.pallas_call(
        _ssd_kernel,
        grid=(batch, nc),
        in_specs=[pl.BlockSpec((L, SSD_D_INNER), lambda b, c: (b * nc + c, 0)),
                  pl.BlockSpec((L, SSD_CONV_DIM), lambda b, c: (b * nc + c, 0)),
                  pl.BlockSpec((L, LANES), lambda b, c: (b * nc + c, 0)),
                  small(LANES), small(LANES), small(SSD_D_INNER), small(SSD_D_INNER)],
        out_specs=pl.BlockSpec((L, SSD_D_INNER), lambda b, c: (b * nc + c, 0)),
        out_shape=jax.ShapeDtypeStruct((batch * seq, SSD_D_INNER), BF16),
        scratch_shapes=[pltpu.VMEM((SSD_GROUPS, SSD_STATE, SSD_GROUP_WIDTH), F32),
                        pltpu.VMEM((L, SSD_D_INNER), F32)],
        compiler_params=_params("arbitrary", "arbitrary"),
        name="ssd_scan",
    )(z, xbc, dt_raw, dtb, alog, dskip, norm_g.reshape(1, SSD_D_INNER))


def _sb_head_block(qh, kb, carry, suffix_mat, strict):
    tk = kb.shape[0]
    l = lax.dot_general(qh, kb, (((1,), (1,)), ((), ())), preferred_element_type=F32)
    neg_abs = lax.bitcast_convert_type(lax.bitcast_convert_type(l, jnp.uint32) | jnp.uint32(0x80000000), F32)
    lg = jnp.log2(1.0 + jnp.exp2(neg_abs))
    log_beta = jnp.minimum(l, 0.0) - lg
    log_fail = log_beta - l
    if strict is not None:
        log_fail = jnp.where(strict, log_fail, 0.0)
    inner = jnp.dot(log_fail.astype(BF16), suffix_mat, preferred_element_type=F32)
    w = jnp.exp2(log_beta + inner + jnp.concatenate([carry] * (tk // LANES), axis=1))
    if strict is not None:
        w = jnp.where(strict, w, 0.0)
    return w.astype(BF16), carry + jnp.sum(log_fail, axis=-1, keepdims=True)


def _sb_kernel(q_ref, k_ref, v_ref, o_ref, carry_ref, acc_ref):
    tq = SB_BLOCK
    i = pl.program_id(2)
    npairs = q_ref.shape[1] // LANES
    suffix_mat = jnp.where(lax.broadcasted_iota(jnp.int32, (tq, tq), 0) > lax.broadcasted_iota(jnp.int32, (tq, tq), 1),
                           1.0, 0.0).astype(BF16)
    lo_lanes = lax.broadcasted_iota(jnp.int32, (tq, LANES), 1) < SB_HEAD_DIM

    def step(j, strict):
        start = pl.multiple_of(j * tq, tq)
        top = None
        for p in range(npairs):
            lanes = slice(p * LANES, (p + 1) * LANES)
            q = q_ref[:, lanes]
            kb = k_ref[pl.ds(start, tq), lanes]
            vb = v_ref[pl.ds(start, tq), lanes]
            zero = jnp.zeros_like(q)
            v_split = jnp.concatenate([jnp.where(lo_lanes, vb, zero), jnp.where(lo_lanes, zero, vb)], axis=0)
            ws = []
            for hh, qh in enumerate((jnp.where(lo_lanes, q, zero), jnp.where(lo_lanes, zero, q))):
                w, carry = _sb_head_block(qh, kb, carry_ref[2 * p + hh], suffix_mat, strict)
                carry_ref[2 * p + hh] = carry
                ws.append(w)
                top = carry if top is None else jnp.maximum(top, carry)
            acc_ref[p] += jnp.dot(jnp.concatenate(ws, axis=1), v_split, preferred_element_type=F32)
        return jnp.max(top)

    carry_ref[...] = jnp.zeros_like(carry_ref)
    acc_ref[...] = jnp.zeros_like(acc_ref)
    qpos = lax.broadcasted_iota(jnp.int32, (tq, tq), 0)
    kpos = lax.broadcasted_iota(jnp.int32, (tq, tq), 1)
    top = step(i, kpos < qpos)

    def live(state):
        j, top = state
        return jnp.logical_and(j >= 0, top > -SB_UNDERFLOW_LOG2)

    def body(state):
        j, _ = state
        return j - 1, step(j, None)

    lax.while_loop(live, body, (i - 1, top))
    for p in range(npairs):
        o_ref[:, p * LANES:(p + 1) * LANES] = acc_ref[p].astype(o_ref.dtype)


def _sb_attention(qkv, batch, seq):
    tq = SB_BLOCK
    nq = seq // tq
    width = SB_PAIRS_PER_STEP * LANES
    nsteps = D_MODEL // width
    return pl.pallas_call(
        _sb_kernel,
        grid=(batch, nsteps, nq),
        in_specs=[pl.BlockSpec((tq, width), lambda b, p, i: (b * nq + i, p)),
                  pl.BlockSpec((seq, width), lambda b, p, i: (b, nsteps + p)),
                  pl.BlockSpec((seq, width), lambda b, p, i: (b, 2 * nsteps + p))],
        out_specs=pl.BlockSpec((tq, width), lambda b, p, i: (b * nq + i, p)),
        out_shape=jax.ShapeDtypeStruct((batch * seq, D_MODEL), BF16),
        scratch_shapes=[pltpu.VMEM((2 * SB_PAIRS_PER_STEP, tq, LANES), F32),
                        pltpu.VMEM((SB_PAIRS_PER_STEP, tq, LANES), F32)],
        compiler_params=_params("arbitrary", "arbitrary", "arbitrary"),
        name="sb_attention",
    )(qkv, qkv, qkv)


def kernel(x, mix_norm, ffn_norm, final_norm, ssd_w_in, ssd_conv_w, ssd_conv_b, ssd_dt_bias, ssd_a_log, ssd_d, ssd_norm, ssd_w_out, sb_w_qkv, sb_w_out, ffn_w_in, ffn_conv_w, ffn_conv_b, ffn_w_out):
    batch, seq, d = x.shape
    xf = x.reshape(batch * seq, d)
    h = _rmsnorm(xf, mix_norm[0])
    out = None
    for i in range(DEPTH):
        j = i // 2
        if i % 2 == 0:
            w_dt = jnp.pad(ssd_w_in[j, :, SSD_D_INNER + SSD_CONV_DIM:], ((0, 0), (0, LANES - SSD_HEADS)))[None]
            z = _mm(h, ssd_w_in, j, 0, SSD_D_INNER, BF16, 1024, 1024, "ssd_z")
            xbc = _mm_conv_act(h, ssd_w_in, j, SSD_D_INNER, ssd_conv_w[j], ssd_conv_b[j], 1, seq, 1024, 1024, "ssd_xbc")
            dt_raw = _mm(h, w_dt, 0, 0, LANES, F32, 1024, LANES, "ssd_dt")
            mixed = _ssd_scan(z, xbc, dt_raw, ssd_dt_bias[j], ssd_a_log[j], ssd_d[j], ssd_norm[j], batch, seq)
            w_out = ssd_w_out
        else:
            qkv = _mm(h, sb_w_qkv, j, 0, 3 * d, BF16, 1024, 1024, "sb_qkv",
                      scaled_tiles=d // 1024, scale=LOG2E * SB_HEAD_DIM ** -0.5)
            mixed = _sb_attention(qkv, batch, seq)
            w_out = sb_w_out
        xf, h = _mm_res_norm(mixed, w_out, j, xf, ffn_norm[i], False, 1024, "mixer_out")
        act = _mm_conv_act(h, ffn_w_in, i, 0, ffn_conv_w[i], ffn_conv_b[i], 2, seq, 512, FFN_D_FF // 2, "ffn_in")
        if i + 1 < DEPTH:
            xf, h = _mm_res_norm(act, ffn_w_out, i, xf, mix_norm[i + 1], False, 512, "ffn_out")
        else:
            out = _mm_res_norm(act, ffn_w_out, i, xf, final_norm, True, 512, "ffn_out_final")
    return out.reshape(batch, seq, d)
```

```python
import functools
import math

import jax
import jax.numpy as jnp
from jax import lax
from jax.experimental import pallas as pl
from jax.experimental.pallas import tpu as pltpu

F32 = jnp.float32
BF16 = jnp.bfloat16

D_MODEL = 1024
DEPTH = 4
NORM_EPS = 1e-6
LOG2E = math.log2(math.e)

SSD_D_INNER = 2048
SSD_HEAD_DIM = 64
SSD_HEADS = 32
SSD_GROUPS = 8
SSD_HEADS_PER_GROUP = 4
SSD_STATE = 128
SSD_CHUNK = 128
SSD_CHUNKS_PER_STEP = 4
SSD_CONV_DIM = SSD_D_INNER + 2 * SSD_GROUPS * SSD_STATE
SSD_GROUP_WIDTH = SSD_HEADS_PER_GROUP * SSD_HEAD_DIM

SB_HEADS = 16
SB_HEAD_DIM = 64
SB_BLOCK = 256
SB_PAIRS_PER_STEP = 4
SB_UNDERFLOW_LOG2 = 160.0

FFN_D_FF = 2816

LANES = 128
SUBLANES = 8
VMEM_LIMIT_BYTES = 56 * 1024 * 1024


def _params(*semantics):
    return pltpu.CompilerParams(dimension_semantics=semantics, vmem_limit_bytes=VMEM_LIMIT_BYTES)


def _rms_scale(v):
    return lax.rsqrt(jnp.mean(v * v, axis=-1, keepdims=True) + NORM_EPS)


def _rmsnorm_kernel(x_ref, g_ref, o_ref):
    x = x_ref[...]
    o_ref[...] = (x * _rms_scale(x) * g_ref[...]).astype(o_ref.dtype)


def _rmsnorm(x, g, tm=1024):
    t, d = x.shape
    return pl.pallas_call(
        _rmsnorm_kernel,
        grid=(t // tm,),
        in_specs=[pl.BlockSpec((tm, d), lambda i: (i, 0)), pl.BlockSpec((1, d), lambda i: (0, 0))],
        out_specs=pl.BlockSpec((tm, d), lambda i: (i, 0)),
        out_shape=jax.ShapeDtypeStruct((t, d), BF16),
        compiler_params=_params("arbitrary"),
        name="rmsnorm",
    )(x, g.reshape(1, d))


def _mm_kernel(h_ref, w_ref, o_ref, wb_ref, *, scaled_tiles, scale):
    @pl.when(pl.program_id(1) == 0)
    def _():
        w = w_ref[...]
        if scaled_tiles:
            w = w * jnp.where(pl.program_id(0) < scaled_tiles, scale, 1.0)
        wb_ref[...] = w.astype(BF16)

    o_ref[...] = jnp.dot(h_ref[...], wb_ref[...], preferred_element_type=F32).astype(o_ref.dtype)


def _mm(h, w, layer, col0, n, out_dtype, tm, tn, name, scaled_tiles=0, scale=1.0):
    t, k = h.shape
    off = col0 // tn
    return pl.pallas_call(
        functools.partial(_mm_kernel, scaled_tiles=scaled_tiles, scale=scale),
        grid=(n // tn, t // tm),
        in_specs=[pl.BlockSpec((tm, k), lambda j, i: (i, 0)),
                  pl.BlockSpec((None, k, tn), lambda j, i: (layer, 0, j + off), pipeline_mode=pl.Buffered(1))],
        out_specs=pl.BlockSpec((tm, tn), lambda j, i: (i, j)),
        out_shape=jax.ShapeDtypeStruct((t, n), out_dtype),
        scratch_shapes=[pltpu.VMEM((k, tn), BF16)],
        compiler_params=_params("arbitrary", "arbitrary"),
        name=name,
    )(h, w)


def _causal_conv(u3, prev, cw, cb):
    width = cw.shape[0]
    n = u3.shape[2]
    before = jnp.concatenate([prev, u3[:-1]], axis=0)
    sub = lax.broadcasted_iota(jnp.int32, (1, SUBLANES, n), 1)
    y = cb.reshape(1, 1, n) + cw[width - 1:width, :].reshape(1, 1, n) * u3
    for j in range(1, width):
        mixed = jnp.where(sub >= SUBLANES - j, before, u3)
        y = y + cw[width - 1 - j:width - j, :].reshape(1, 1, n) * pltpu.roll(mixed, j, axis=1)
    return y


def _silu(y):
    return y * (1.0 / (1.0 + jnp.exp2(y * -LOG2E)))


def _mm_conv_act_kernel(*refs, nbranch, tiles_per_seq):
    h_ref = refs[0]
    w_refs = refs[1:1 + nbranch]
    cw_refs = refs[1 + nbranch:1 + 2 * nbranch]
    cb_refs = refs[1 + 2 * nbranch:1 + 3 * nbranch]
    o_ref = refs[1 + 3 * nbranch]
    prev_refs = refs[2 + 3 * nbranch:2 + 4 * nbranch]
    wb_refs = refs[2 + 4 * nbranch:]
    tm = h_ref.shape[0]

    @pl.when(pl.program_id(1) == 0)
    def _():
        for w_ref, wb_ref in zip(w_refs, wb_refs):
            wb_ref[...] = w_ref[...].astype(BF16)

    @pl.when(pl.program_id(1) % tiles_per_seq == 0)
    def _():
        for prev_ref in prev_refs:
            prev_ref[...] = jnp.zeros_like(prev_ref)

    h = h_ref[...]
    ys = []
    for b in range(nbranch):
        u = jnp.dot(h, wb_refs[b][...], preferred_element_type=F32)
        u3 = u.reshape(tm // SUBLANES, SUBLANES, u.shape[1])
        ys.append(_causal_conv(u3, prev_refs[b][...], cw_refs[b][...], cb_refs[b][...]))
        prev_refs[b][...] = u3[tm // SUBLANES - 1:]
    act = _silu(ys[0])
    if nbranch == 2:
        act = act * ys[1]
    o_ref[...] = act.reshape(o_ref.shape).astype(o_ref.dtype)


def _mm_conv_act(h, w, layer, col0, cw, cb, nbranch, seq, tm, tn, name):
    t, k = h.shape
    n_out = cw.shape[1] // nbranch
    width = cw.shape[0]
    ncol = n_out // tn
    off = col0 // tn
    in_specs = [pl.BlockSpec((tm, k), lambda j, i: (i, 0))]
    in_specs += [pl.BlockSpec((None, k, tn), lambda j, i, b=b: (layer, 0, j + b * ncol + off),
                              pipeline_mode=pl.Buffered(1)) for b in range(nbranch)]
    in_specs += [pl.BlockSpec((width, tn), lambda j, i, b=b: (0, j + b * ncol)) for b in range(nbranch)]
    in_specs += [pl.BlockSpec((1, tn), lambda j, i, b=b: (0, j + b * ncol)) for b in range(nbranch)]
    cb2 = cb.reshape(1, -1)
    return pl.pallas_call(
        functools.partial(_mm_conv_act_kernel, nbranch=nbranch, tiles_per_seq=seq // tm),
        grid=(ncol, t // tm),
        in_specs=in_specs,
        out_specs=pl.BlockSpec((tm, tn), lambda j, i: (i, j)),
        out_shape=jax.ShapeDtypeStruct((t, n_out), BF16),
        scratch_shapes=[pltpu.VMEM((1, SUBLANES, tn), F32)] * nbranch + [pltpu.VMEM((k, tn), BF16)] * nbranch,
        compiler_params=_params("arbitrary", "arbitrary"),
        name=name,
    )(h, *([w] * nbranch), *([cw] * nbranch), *([cb2] * nbranch))


def _residual_update(a_ref, w_ref, x_ref, wb_ref):
    @pl.when(pl.program_id(0) == 0)
    def _():
        wb_ref[...] = w_ref[...].astype(BF16)

    return x_ref[...] + jnp.dot(a_ref[...], wb_ref[...], preferred_element_type=F32)


def _mm_res_norm_kernel(a_ref, w_ref, x_ref, g_ref, xo_ref, ho_ref, wb_ref):
    xn = _residual_update(a_ref, w_ref, x_ref, wb_ref)
    xo_ref[...] = xn
    ho_ref[...] = (xn * _rms_scale(xn) * g_ref[...]).astype(ho_ref.dtype)


def _mm_res_final_kernel(a_ref, w_ref, x_ref, g_ref, o_ref, wb_ref):
    xn = _residual_update(a_ref, w_ref, x_ref, wb_ref)
    o_ref[...] = xn * _rms_scale(xn) * g_ref[...]


def _mm_res_norm(a, w, layer, x, g, final, tm, name):
    t, k = a.shape
    d = w.shape[2]
    row = pl.BlockSpec((tm, d), lambda i: (i, 0))
    in_specs = [pl.BlockSpec((tm, k), lambda i: (i, 0)),
                pl.BlockSpec((None, k, d), lambda i: (layer, 0, 0), pipeline_mode=pl.Buffered(1)),
                row, pl.BlockSpec((1, d), lambda i: (0, 0))]
    scratch = [pltpu.VMEM((k, d), BF16)]
    if final:
        return pl.pallas_call(
            _mm_res_final_kernel, grid=(t // tm,), in_specs=in_specs, out_specs=row,
            out_shape=jax.ShapeDtypeStruct((t, d), F32), scratch_shapes=scratch,
            compiler_params=_params("arbitrary"), name=name,
        )(a, w, x, g.reshape(1, d))
    return pl.pallas_call(
        _mm_res_norm_kernel, grid=(t // tm,), in_specs=in_specs, out_specs=[row, row],
        out_shape=[jax.ShapeDtypeStruct((t, d), F32), jax.ShapeDtypeStruct((t, d), BF16)], scratch_shapes=scratch,
        compiler_params=_params("arbitrary"), name=name,
    )(a, w, x, g.reshape(1, d))


def _split3(v):
    a = v.astype(BF16)
    r = v - a.astype(F32)
    b = r.astype(BF16)
    c = (r - b.astype(F32)).astype(BF16)
    return a, b, c


def _expand_heads(v, h0, nheads):
    rows = v.shape[0]
    lane = lax.broadcasted_iota(jnp.int32, (rows, LANES), 1)
    pairs = []
    for p in range(nheads // 2):
        if rows % SUBLANES:
            lo = jnp.broadcast_to(v[:, h0 + 2 * p:h0 + 2 * p + 1], (rows, LANES))
            hi = jnp.broadcast_to(v[:, h0 + 2 * p + 1:h0 + 2 * p + 2], (rows, LANES))
            pairs.append(jnp.where(lane < SSD_HEAD_DIM, lo, hi))
        else:
            pairs.append(jnp.take_along_axis(v, h0 + 2 * p + lane // SSD_HEAD_DIM, axis=1))
    return jnp.concatenate(pairs, axis=1)


def _ssd_kernel(z_ref, xbc_ref, dt_ref, dtb_ref, alog_ref, dskip_ref, ng_ref, o_ref, state_ref, y_ref):
    @pl.when(pl.program_id(1) == 0)
    def _():
        state_ref[...] = jnp.zeros_like(state_ref)

    for c in range(SSD_CHUNKS_PER_STEP):
        _ssd_chunk(slice(c * SSD_CHUNK, (c + 1) * SSD_CHUNK), z_ref, xbc_ref, dt_ref, dtb_ref, alog_ref, dskip_ref,
                   ng_ref, o_ref, state_ref, y_ref)


def _ssd_chunk(rows, z_ref, xbc_ref, dt_ref, dtb_ref, alog_ref, dskip_ref, ng_ref, o_ref, state_ref, y_ref):
    L = SSD_CHUNK
    pre = dt_ref[rows, :] + dtb_ref[...]
    dt = jnp.maximum(pre, 0.0) + jnp.log1p(jnp.exp(-jnp.abs(pre)))
    a = (-LOG2E * jnp.exp(alog_ref[...])) * dt
    row = lax.broadcasted_iota(jnp.int32, (L, L), 0)
    col = lax.broadcasted_iota(jnp.int32, (L, L), 1)
    causal = row >= col
    tri = jnp.where(causal, 1.0, 0.0).astype(BF16)
    a1, a2, a3 = _split3(a)
    acum = (jnp.dot(tri, a1, preferred_element_type=F32) + jnp.dot(tri, a2, preferred_element_type=F32)
            + jnp.dot(tri, a3, preferred_element_type=F32))
    src_t = (acum - jnp.log2(jnp.maximum(dt, jnp.finfo(F32).tiny))).T
    last = acum[L - 1:L, :]
    exp_acum = jnp.exp2(acum)
    w_dt = jnp.exp2(last - acum) * dt
    exp_last = jnp.exp2(last)
    lane = lax.broadcasted_iota(jnp.int32, (L, LANES), 1)
    lo_lanes = lane < SSD_HEAD_DIM

    for g in range(SSD_GROUPS):
        b_g = xbc_ref[rows, SSD_D_INNER + g * SSD_STATE:SSD_D_INNER + (g + 1) * SSD_STATE]
        c_off = SSD_D_INNER + SSD_GROUPS * SSD_STATE
        c_g = xbc_ref[rows, c_off + g * SSD_STATE:c_off + (g + 1) * SSD_STATE]
        cb = lax.dot_general(c_g, b_g, (((1,), (1,)), ((), ())), preferred_element_type=F32)
        x0 = g * SSD_GROUP_WIDTH
        h0 = g * SSD_HEADS_PER_GROUP
        y_diag = []
        for p in range(SSD_HEADS_PER_GROUP // 2):
            scores = []
            for hh in (h0 + 2 * p, h0 + 2 * p + 1):
                seg = acum[:, hh:hh + 1] - src_t[hh:hh + 1, :]
                scores.append((cb * jnp.exp2(jnp.where(causal, seg, -jnp.inf))).astype(BF16))
            xs = xbc_ref[rows, x0 + p * LANES:x0 + (p + 1) * LANES]
            zero = jnp.zeros_like(xs)
            xs_split = jnp.concatenate([jnp.where(lo_lanes, xs, zero), jnp.where(lo_lanes, zero, xs)], axis=0)
            y_diag.append(jnp.dot(jnp.concatenate(scores, axis=1), xs_split, preferred_element_type=F32))
        y_diag = jnp.concatenate(y_diag, axis=1)

        xs_g = xbc_ref[rows, x0:x0 + SSD_GROUP_WIDTH].astype(F32)
        state = state_ref[g]
        y_off = (jnp.dot(c_g, state.astype(BF16), preferred_element_type=F32)
                 * _expand_heads(exp_acum, h0, SSD_HEADS_PER_GROUP))
        xw = (xs_g * _expand_heads(w_dt, h0, SSD_HEADS_PER_GROUP)).astype(BF16)
        state_ref[g] = (state * _expand_heads(exp_last, h0, SSD_HEADS_PER_GROUP)
                        + lax.dot_general(b_g, xw, (((0,), (0,)), ((), ())), preferred_element_type=F32))
        y_ref[rows, x0:x0 + SSD_GROUP_WIDTH] = y_diag + y_off + dskip_ref[:, x0:x0 + SSD_GROUP_WIDTH] * xs_g

    z = z_ref[rows, :].astype(F32)
    gated = y_ref[rows, :] * _silu(z)
    o_ref[rows, :] = (gated * _rms_scale(gated) * ng_ref[...]).astype(o_ref.dtype)


def _ssd_scan(z, xbc, dt_raw, dt_bias, a_log, d_skip, norm_g, batch, seq):
    L = SSD_CHUNKS_PER_STEP * SSD_CHUNK
    nc = seq // L
    pad = LANES - SSD_HEADS
    dtb = jnp.pad(dt_bias, (0, pad)).reshape(1, LANES)
    alog = jnp.pad(a_log, (0, pad)).reshape(1, LANES)
    dskip = jnp.repeat(d_skip, SSD_HEAD_DIM).reshape(1, SSD_D_INNER)
    small = lambda n: pl.BlockSpec((1, n), lambda b, c: (0, 0))
    return pl.pallas_call(
        _ssd_kernel,
        grid=(batch, nc),
        in_specs=[pl.BlockSpec((L, SSD_D_INNER), lambda b, c: (b * nc + c, 0)),
                  pl.BlockSpec((L, SSD_CONV_DIM), lambda b, c: (b * nc + c, 0)),
                  pl.BlockSpec((L, LANES), lambda b, c: (b * nc + c, 0)),
                  small(LANES), small(LANES), small(SSD_D_INNER), small(SSD_D_INNER)],
        out_specs=pl.BlockSpec((L, SSD_D_INNER), lambda b, c: (b * nc + c, 0)),
        out_shape=jax.ShapeDtypeStruct((batch * seq, SSD_D_INNER), BF16),
        scratch_shapes=[pltpu.VMEM((SSD_GROUPS, SSD_STATE, SSD_GROUP_WIDTH), F32),
                        pltpu.VMEM((L, SSD_D_INNER), F32)],
        compiler_params=_params("arbitrary", "arbitrary"),
        name="ssd_scan",
    )(z, xbc, dt_raw, dtb, alog, dskip, norm_g.reshape(1, SSD_D_INNER))


def _sb_head_block(qh, kb, carry, suffix_mat, strict):
    tk = kb.shape[0]
    l = lax.dot_general(qh, kb, (((1,), (1,)), ((), ())), preferred_element_type=F32)
    neg_abs = lax.bitcast_convert_type(lax.bitcast_convert_type(l, jnp.uint32) | jnp.uint32(0x80000000), F32)
    lg = jnp.log2(1.0 + jnp.exp2(neg_abs))
    log_beta = jnp.minimum(l, 0.0) - lg
    log_fail = log_beta - l
    if strict is not None:
        log_fail = jnp.where(strict, log_fail, 0.0)
    inner = jnp.dot(log_fail.astype(BF16), suffix_mat, preferred_element_type=F32)
    w = jnp.exp2(log_beta + inner + jnp.concatenate([carry] * (tk // LANES), axis=1))
    if strict is not None:
        w = jnp.where(strict, w, 0.0)
    return w.astype(BF16), carry + jnp.sum(log_fail, axis=-1, keepdims=True)


def _sb_kernel(q_ref, k_ref, v_ref, o_ref, carry_ref, acc_ref):
    tq = SB_BLOCK
    i = pl.program_id(2)
    npairs = q_ref.shape[1] // LANES
    suffix_mat = jnp.where(lax.broadcasted_iota(jnp.int32, (tq, tq), 0) > lax.broadcasted_iota(jnp.int32, (tq, tq), 1),
                           1.0, 0.0).astype(BF16)
    lo_lanes = lax.broadcasted_iota(jnp.int32, (tq, LANES), 1) < SB_HEAD_DIM

    def step(j, strict):
        start = pl.multiple_of(j * tq, tq)
        top = None
        for p in range(npairs):
            lanes = slice(p * LANES, (p + 1) * LANES)
            q = q_ref[:, lanes]
            kb = k_ref[pl.ds(start, tq), lanes]
            vb = v_ref[pl.ds(start, tq), lanes]
            zero = jnp.zeros_like(q)
            v_split = jnp.concatenate([jnp.where(lo_lanes, vb, zero), jnp.where(lo_lanes, zero, vb)], axis=0)
            ws = []
            for hh, qh in enumerate((jnp.where(lo_lanes, q, zero), jnp.where(lo_lanes, zero, q))):
                w, carry = _sb_head_block(qh, kb, carry_ref[2 * p + hh], suffix_mat, strict)
                carry_ref[2 * p + hh] = carry
                ws.append(w)
                top = carry if top is None else jnp.maximum(top, carry)
            acc_ref[p] += jnp.dot(jnp.concatenate(ws, axis=1), v_split, preferred_element_type=F32)
        return jnp.max(top)

    carry_ref[...] = jnp.zeros_like(carry_ref)
    acc_ref[...] = jnp.zeros_like(acc_ref)
    qpos = lax.broadcasted_iota(jnp.int32, (tq, tq), 0)
    kpos = lax.broadcasted_iota(jnp.int32, (tq, tq), 1)
    strict = kpos < qpos

    def first_two():
        step(i, strict)
        return step(i - 1, None)

    top = lax.cond(i > 0, first_two, lambda: step(i, strict))

    def live(state):
        j, top = state
        return jnp.logical_and(j >= 0, top > -SB_UNDERFLOW_LOG2)

    def body(state):
        j, _ = state
        return j - 1, step(j, None)

    lax.while_loop(live, body, (i - 2, top))
    for p in range(npairs):
        o_ref[:, p * LANES:(p + 1) * LANES] = acc_ref[p].astype(o_ref.dtype)


def _sb_attention(qkv, batch, seq):
    tq = SB_BLOCK
    nq = seq // tq
    width = SB_PAIRS_PER_STEP * LANES
    nsteps = D_MODEL // width
    return pl.pallas_call(
        _sb_kernel,
        grid=(batch, nsteps, nq),
        in_specs=[pl.BlockSpec((tq, width), lambda b, p, i: (b * nq + i, p)),
                  pl.BlockSpec((seq, width), lambda b, p, i: (b, nsteps + p)),
                  pl.BlockSpec((seq, width), lambda b, p, i: (b, 2 * nsteps + p))],
        out_specs=pl.BlockSpec((tq, width), lambda b, p, i: (b * nq + i, p)),
        out_shape=jax.ShapeDtypeStruct((batch * seq, D_MODEL), BF16),
        scratch_shapes=[pltpu.VMEM((2 * SB_PAIRS_PER_STEP, tq, LANES), F32),
                        pltpu.VMEM((SB_PAIRS_PER_STEP, tq, LANES), F32)],
        compiler_params=_params("arbitrary", "arbitrary", "arbitrary"),
        name="sb_attention",
    )(qkv, qkv, qkv)


def kernel(x, mix_norm, ffn_norm, final_norm, ssd_w_in, ssd_conv_w, ssd_conv_b, ssd_dt_bias, ssd_a_log, ssd_d, ssd_norm, ssd_w_out, sb_w_qkv, sb_w_out, ffn_w_in, ffn_conv_w, ffn_conv_b, ffn_w_out):
    batch, seq, d = x.shape
    xf = x.reshape(batch * seq, d)
    h = _rmsnorm(xf, mix_norm[0])
    out = None
    for i in range(DEPTH):
        j = i // 2
        if i % 2 == 0:
            w_dt = jnp.pad(ssd_w_in[j, :, SSD_D_INNER + SSD_CONV_DIM:], ((0, 0), (0, LANES - SSD_HEADS)))[None]
            z = _mm(h, ssd_w_in, j, 0, SSD_D_INNER, BF16, 1024, 1024, "ssd_z")
            xbc = _mm_conv_act(h, ssd_w_in, j, SSD_D_INNER, ssd_conv_w[j], ssd_conv_b[j], 1, seq, 1024, 1024, "ssd_xbc")
            dt_raw = _mm(h, w_dt, 0, 0, LANES, F32, 1024, LANES, "ssd_dt")
            mixed = _ssd_scan(z, xbc, dt_raw, ssd_dt_bias[j], ssd_a_log[j], ssd_d[j], ssd_norm[j], batch, seq)
            w_out = ssd_w_out
        else:
            qkv = _mm(h, sb_w_qkv, j, 0, 3 * d, BF16, 1024, 1024, "sb_qkv",
                      scaled_tiles=d // 1024, scale=LOG2E * SB_HEAD_DIM ** -0.5)
            mixed = _sb_attention(qkv, batch, seq)
            w_out = sb_w_out
        xf, h = _mm_res_norm(mixed, w_out, j, xf, ffn_norm[i], False, 1024, "mixer_out")
        act = _mm_conv_act(h, ffn_w_in, i, 0, ffn_conv_w[i], ffn_conv_b[i], 2, seq, 512, FFN_D_FF // 2, "ffn_in")
        if i + 1 < DEPTH:
            xf, h = _mm_res_norm(act, ffn_w_out, i, xf, mix_norm[i + 1], False, 1024, "ffn_out")
        else:
            out = _mm_res_norm(act, ffn_w_out, i, xf, final_norm, True, 1024, "ffn_out_final")
    return out.reshape(batch, seq, d)
```

```python
import functools
import math

import jax
import jax.numpy as jnp
from jax import lax
from jax.experimental import pallas as pl
from jax.experimental.pallas import tpu as pltpu

F32 = jnp.float32
BF16 = jnp.bfloat16

D_MODEL = 1024
DEPTH = 4
NORM_EPS = 1e-6
LOG2E = math.log2(math.e)

SSD_D_INNER = 2048
SSD_HEAD_DIM = 64
SSD_HEADS = 32
SSD_GROUPS = 8
SSD_HEADS_PER_GROUP = 4
SSD_STATE = 128
SSD_CHUNK = 128
SSD_CHUNKS_PER_STEP = 4
SSD_CONV_DIM = SSD_D_INNER + 2 * SSD_GROUPS * SSD_STATE
SSD_GROUP_WIDTH = SSD_HEADS_PER_GROUP * SSD_HEAD_DIM

SB_HEADS = 16
SB_HEAD_DIM = 64
SB_BLOCK = 256
SB_PAIRS_PER_STEP = 4
SB_UNDERFLOW_LOG2 = 160.0

FFN_D_FF = 2816

LANES = 128
SUBLANES = 8
VMEM_LIMIT_BYTES = 56 * 1024 * 1024


def _params(*semantics):
    return pltpu.CompilerParams(dimension_semantics=semantics, vmem_limit_bytes=VMEM_LIMIT_BYTES)


def _rms_scale(v):
    return lax.rsqrt(jnp.mean(v * v, axis=-1, keepdims=True) + NORM_EPS)


def _rmsnorm_kernel(x_ref, g_ref, o_ref):
    x = x_ref[...]
    o_ref[...] = (x * _rms_scale(x) * g_ref[...]).astype(o_ref.dtype)


def _rmsnorm(x, g, tm=1024):
    t, d = x.shape
    return pl.pallas_call(
        _rmsnorm_kernel,
        grid=(t // tm,),
        in_specs=[pl.BlockSpec((tm, d), lambda i: (i, 0)), pl.BlockSpec((1, d), lambda i: (0, 0))],
        out_specs=pl.BlockSpec((tm, d), lambda i: (i, 0)),
        out_shape=jax.ShapeDtypeStruct((t, d), BF16),
        compiler_params=_params("arbitrary"),
        name="rmsnorm",
    )(x, g.reshape(1, d))


def _mm_kernel(h_ref, w_ref, o_ref, wb_ref, *, scaled_tiles, scale):
    @pl.when(pl.program_id(1) == 0)
    def _():
        w = w_ref[...]
        if scaled_tiles:
            w = w * jnp.where(pl.program_id(0) < scaled_tiles, scale, 1.0)
        wb_ref[...] = w.astype(BF16)

    o_ref[...] = jnp.dot(h_ref[...], wb_ref[...], preferred_element_type=F32).astype(o_ref.dtype)


def _mm(h, w, layer, col0, n, out_dtype, tm, tn, name, scaled_tiles=0, scale=1.0):
    t, k = h.shape
    off = col0 // tn
    return pl.pallas_call(
        functools.partial(_mm_kernel, scaled_tiles=scaled_tiles, scale=scale),
        grid=(n // tn, t // tm),
        in_specs=[pl.BlockSpec((tm, k), lambda j, i: (i, 0)),
                  pl.BlockSpec((None, k, tn), lambda j, i: (layer, 0, j + off), pipeline_mode=pl.Buffered(1))],
        out_specs=pl.BlockSpec((tm, tn), lambda j, i: (i, j)),
        out_shape=jax.ShapeDtypeStruct((t, n), out_dtype),
        scratch_shapes=[pltpu.VMEM((k, tn), BF16)],
        compiler_params=_params("arbitrary", "arbitrary"),
        name=name,
    )(h, w)


def _causal_conv(u3, prev, cw, cb):
    width = cw.shape[0]
    n = u3.shape[2]
    before = jnp.concatenate([prev, u3[:-1]], axis=0)
    sub = lax.broadcasted_iota(jnp.int32, (1, SUBLANES, n), 1)
    y = cb.reshape(1, 1, n) + cw[width - 1:width, :].reshape(1, 1, n) * u3
    for j in range(1, width):
        mixed = jnp.where(sub >= SUBLANES - j, before, u3)
        y = y + cw[width - 1 - j:width - j, :].reshape(1, 1, n) * pltpu.roll(mixed, j, axis=1)
    return y


def _silu(y):
    return y * (1.0 / (1.0 + jnp.exp2(y * -LOG2E)))


def _mm_conv_act_kernel(*refs, nbranch, tiles_per_seq):
    h_ref = refs[0]
    w_refs = refs[1:1 + nbranch]
    cw_refs = refs[1 + nbranch:1 + 2 * nbranch]
    cb_refs = refs[1 + 2 * nbranch:1 + 3 * nbranch]
    o_ref = refs[1 + 3 * nbranch]
    prev_refs = refs[2 + 3 * nbranch:2 + 4 * nbranch]
    wb_ref = refs[2 + 4 * nbranch]
    tm = h_ref.shape[0]
    tn = o_ref.shape[1]

    @pl.when(pl.program_id(1) == 0)
    def _():
        for b, w_ref in enumerate(w_refs):
            wb_ref[:, b * tn:(b + 1) * tn] = w_ref[...].astype(BF16)

    @pl.when(pl.program_id(1) % tiles_per_seq == 0)
    def _():
        for prev_ref in prev_refs:
            prev_ref[...] = jnp.zeros_like(prev_ref)

    u_all = jnp.dot(h_ref[...], wb_ref[...], preferred_element_type=F32)
    ys = []
    for b in range(nbranch):
        u = u_all[:, b * tn:(b + 1) * tn]
        u3 = u.reshape(tm // SUBLANES, SUBLANES, u.shape[1])
        ys.append(_causal_conv(u3, prev_refs[b][...], cw_refs[b][...], cb_refs[b][...]))
        prev_refs[b][...] = u3[tm // SUBLANES - 1:]
    act = _silu(ys[0])
    if nbranch == 2:
        act = act * ys[1]
    o_ref[...] = act.reshape(o_ref.shape).astype(o_ref.dtype)


def _mm_conv_act(h, w, layer, col0, cw, cb, nbranch, seq, tm, tn, name):
    t, k = h.shape
    n_out = cw.shape[1] // nbranch
    width = cw.shape[0]
    ncol = n_out // tn
    off = col0 // tn
    in_specs = [pl.BlockSpec((tm, k), lambda j, i: (i, 0))]
    in_specs += [pl.BlockSpec((None, k, tn), lambda j, i, b=b: (layer, 0, j + b * ncol + off),
                              pipeline_mode=pl.Buffered(1)) for b in range(nbranch)]
    in_specs += [pl.BlockSpec((width, tn), lambda j, i, b=b: (0, j + b * ncol)) for b in range(nbranch)]
    in_specs += [pl.BlockSpec((1, tn), lambda j, i, b=b: (0, j + b * ncol)) for b in range(nbranch)]
    cb2 = cb.reshape(1, -1)
    return pl.pallas_call(
        functools.partial(_mm_conv_act_kernel, nbranch=nbranch, tiles_per_seq=seq // tm),
        grid=(ncol, t // tm),
        in_specs=in_specs,
        out_specs=pl.BlockSpec((tm, tn), lambda j, i: (i, j)),
        out_shape=jax.ShapeDtypeStruct((t, n_out), BF16),
        scratch_shapes=[pltpu.VMEM((1, SUBLANES, tn), F32)] * nbranch + [pltpu.VMEM((k, nbranch * tn), BF16)],
        compiler_params=_params("arbitrary", "arbitrary"),
        name=name,
    )(h, *([w] * nbranch), *([cw] * nbranch), *([cb2] * nbranch))


def _residual_update(a_ref, w_ref, x_ref, wb_ref):
    @pl.when(pl.program_id(0) == 0)
    def _():
        wb_ref[...] = w_ref[...].astype(BF16)

    return x_ref[...] + jnp.dot(a_ref[...], wb_ref[...], preferred_element_type=F32)


def _mm_res_norm_kernel(a_ref, w_ref, x_ref, g_ref, xo_ref, ho_ref, wb_ref):
    xn = _residual_update(a_ref, w_ref, x_ref, wb_ref)
    xo_ref[...] = xn
    ho_ref[...] = (xn * _rms_scale(xn) * g_ref[...]).astype(ho_ref.dtype)


def _mm_res_final_kernel(a_ref, w_ref, x_ref, g_ref, o_ref, wb_ref):
    xn = _residual_update(a_ref, w_ref, x_ref, wb_ref)
    o_ref[...] = xn * _rms_scale(xn) * g_ref[...]


def _mm_res_norm(a, w, layer, x, g, final, tm, name):
    t, k = a.shape
    d = w.shape[2]
    row = pl.BlockSpec((tm, d), lambda i: (i, 0))
    in_specs = [pl.BlockSpec((tm, k), lambda i: (i, 0)),
                pl.BlockSpec((None, k, d), lambda i: (layer, 0, 0), pipeline_mode=pl.Buffered(1)),
                row, pl.BlockSpec((1, d), lambda i: (0, 0))]
    scratch = [pltpu.VMEM((k, d), BF16)]
    if final:
        return pl.pallas_call(
            _mm_res_final_kernel, grid=(t // tm,), in_specs=in_specs, out_specs=row,
            out_shape=jax.ShapeDtypeStruct((t, d), F32), scratch_shapes=scratch,
            compiler_params=_params("arbitrary"), name=name,
        )(a, w, x, g.reshape(1, d))
    return pl.pallas_call(
        _mm_res_norm_kernel, grid=(t // tm,), in_specs=in_specs, out_specs=[row, row],
        out_shape=[jax.ShapeDtypeStruct((t, d), F32), jax.ShapeDtypeStruct((t, d), BF16)], scratch_shapes=scratch,
        compiler_params=_params("arbitrary"), name=name,
    )(a, w, x, g.reshape(1, d))


def _split3(v):
    a = v.astype(BF16)
    r = v - a.astype(F32)
    b = r.astype(BF16)
    c = (r - b.astype(F32)).astype(BF16)
    return a, b, c


def _expand_heads(v, h0, nheads):
    rows = v.shape[0]
    lane = lax.broadcasted_iota(jnp.int32, (rows, LANES), 1)
    pairs = []
    for p in range(nheads // 2):
        if rows % SUBLANES:
            lo = jnp.broadcast_to(v[:, h0 + 2 * p:h0 + 2 * p + 1], (rows, LANES))
            hi = jnp.broadcast_to(v[:, h0 + 2 * p + 1:h0 + 2 * p + 2], (rows, LANES))
            pairs.append(jnp.where(lane < SSD_HEAD_DIM, lo, hi))
        else:
            pairs.append(jnp.take_along_axis(v, h0 + 2 * p + lane // SSD_HEAD_DIM, axis=1))
    return jnp.concatenate(pairs, axis=1)


def _ssd_kernel(z_ref, xbc_ref, dt_ref, dtb_ref, alog_ref, dskip_ref, ng_ref, o_ref, state_ref, y_ref):
    @pl.when(pl.program_id(1) == 0)
    def _():
        state_ref[...] = jnp.zeros_like(state_ref)

    for c in range(SSD_CHUNKS_PER_STEP):
        _ssd_chunk(slice(c * SSD_CHUNK, (c + 1) * SSD_CHUNK), z_ref, xbc_ref, dt_ref, dtb_ref, alog_ref, dskip_ref,
                   ng_ref, o_ref, state_ref, y_ref)


def _ssd_chunk(rows, z_ref, xbc_ref, dt_ref, dtb_ref, alog_ref, dskip_ref, ng_ref, o_ref, state_ref, y_ref):
    L = SSD_CHUNK
    pre = dt_ref[rows, :] + dtb_ref[...]
    dt = jnp.maximum(pre, 0.0) + jnp.log1p(jnp.exp(-jnp.abs(pre)))
    a = (-LOG2E * jnp.exp(alog_ref[...])) * dt
    row = lax.broadcasted_iota(jnp.int32, (L, L), 0)
    col = lax.broadcasted_iota(jnp.int32, (L, L), 1)
    causal = row >= col
    tri = jnp.where(causal, 1.0, 0.0).astype(BF16)
    a1, a2, a3 = _split3(a)
    acum = (jnp.dot(tri, a1, preferred_element_type=F32) + jnp.dot(tri, a2, preferred_element_type=F32)
            + jnp.dot(tri, a3, preferred_element_type=F32))
    src_t = (acum - jnp.log2(jnp.maximum(dt, jnp.finfo(F32).tiny))).T
    last = acum[L - 1:L, :]
    exp_acum = jnp.exp2(acum)
    w_dt = jnp.exp2(last - acum) * dt
    exp_last = jnp.exp2(last)
    lane = lax.broadcasted_iota(jnp.int32, (L, LANES), 1)
    lo_lanes = lane < SSD_HEAD_DIM

    for g in range(SSD_GROUPS):
        b_g = xbc_ref[rows, SSD_D_INNER + g * SSD_STATE:SSD_D_INNER + (g + 1) * SSD_STATE]
        c_off = SSD_D_INNER + SSD_GROUPS * SSD_STATE
        c_g = xbc_ref[rows, c_off + g * SSD_STATE:c_off + (g + 1) * SSD_STATE]
        cb = lax.dot_general(c_g, b_g, (((1,), (1,)), ((), ())), preferred_element_type=F32)
        x0 = g * SSD_GROUP_WIDTH
        h0 = g * SSD_HEADS_PER_GROUP
        y_diag = []
        for p in range(SSD_HEADS_PER_GROUP // 2):
            scores = []
            for hh in (h0 + 2 * p, h0 + 2 * p + 1):
                seg = acum[:, hh:hh + 1] - src_t[hh:hh + 1, :]
                scores.append((cb * jnp.exp2(jnp.where(causal, seg, -jnp.inf))).astype(BF16))
            xs = xbc_ref[rows, x0 + p * LANES:x0 + (p + 1) * LANES]
            zero = jnp.zeros_like(xs)
            xs_split = jnp.concatenate([jnp.where(lo_lanes, xs, zero), jnp.where(lo_lanes, zero, xs)], axis=0)
            y_diag.append(jnp.dot(jnp.concatenate(scores, axis=1), xs_split, preferred_element_type=F32))
        y_diag = jnp.concatenate(y_diag, axis=1)

        xs_g = xbc_ref[rows, x0:x0 + SSD_GROUP_WIDTH].astype(F32)
        state = state_ref[g]
        y_off = (jnp.dot(c_g, state.astype(BF16), preferred_element_type=F32)
                 * _expand_heads(exp_acum, h0, SSD_HEADS_PER_GROUP))
        xw = (xs_g * _expand_heads(w_dt, h0, SSD_HEADS_PER_GROUP)).astype(BF16)
        state_ref[g] = (state * _expand_heads(exp_last, h0, SSD_HEADS_PER_GROUP)
                        + lax.dot_general(b_g, xw, (((0,), (0,)), ((), ())), preferred_element_type=F32))
        y_ref[rows, x0:x0 + SSD_GROUP_WIDTH] = y_diag + y_off + dskip_ref[:, x0:x0 + SSD_GROUP_WIDTH] * xs_g

    z = z_ref[rows, :].astype(F32)
    gated = y_ref[rows, :] * _silu(z)
    o_ref[rows, :] = (gated * _rms_scale(gated) * ng_ref[...]).astype(o_ref.dtype)


def _ssd_scan(z, xbc, dt_raw, dt_bias, a_log, d_skip, norm_g, batch, seq):
    L = SSD_CHUNKS_PER_STEP * SSD_CHUNK
    nc = seq // L
    pad = LANES - SSD_HEADS
    dtb = jnp.pad(dt_bias, (0, pad)).reshape(1, LANES)
    alog = jnp.pad(a_log, (0, pad)).reshape(1, LANES)
    dskip = jnp.repeat(d_skip, SSD_HEAD_DIM).reshape(1, SSD_D_INNER)
    small = lambda n: pl.BlockSpec((1, n), lambda b, c: (0, 0))
    return pl.pallas_call(
        _ssd_kernel,
        grid=(batch, nc),
        in_specs=[pl.BlockSpec((L, SSD_D_INNER), lambda b, c: (b * nc + c, 0)),
                  pl.BlockSpec((L, SSD_CONV_DIM), lambda b, c: (b * nc + c, 0)),
                  pl.BlockSpec((L, LANES), lambda b, c: (b * nc + c, 0)),
                  small(LANES), small(LANES), small(SSD_D_INNER), small(SSD_D_INNER)],
        out_specs=pl.BlockSpec((L, SSD_D_INNER), lambda b, c: (b * nc + c, 0)),
        out_shape=jax.ShapeDtypeStruct((batch * seq, SSD_D_INNER), BF16),
        scratch_shapes=[pltpu.VMEM((SSD_GROUPS, SSD_STATE, SSD_GROUP_WIDTH), F32),
                        pltpu.VMEM((L, SSD_D_INNER), F32)],
        compiler_params=_params("arbitrary", "arbitrary"),
        name="ssd_scan",
    )(z, xbc, dt_raw, dtb, alog, dskip, norm_g.reshape(1, SSD_D_INNER))


def _sb_head_block(qh, kb, carry, suffix_mat, strict):
    tk = kb.shape[0]
    l = lax.dot_general(qh, kb, (((1,), (1,)), ((), ())), preferred_element_type=F32)
    neg_abs = lax.bitcast_convert_type(lax.bitcast_convert_type(l, jnp.uint32) | jnp.uint32(0x80000000), F32)
    lg = jnp.log2(1.0 + jnp.exp2(neg_abs))
    log_beta = jnp.minimum(l, 0.0) - lg
    log_fail = log_beta - l
    if strict is not None:
        log_fail = jnp.where(strict, log_fail, 0.0)
    inner = jnp.dot(log_fail.astype(BF16), suffix_mat, preferred_element_type=F32)
    w = jnp.exp2(log_beta + inner + jnp.concatenate([carry] * (tk // LANES), axis=1))
    if strict is not None:
        w = jnp.where(strict, w, 0.0)
    return w.astype(BF16), carry + jnp.sum(log_fail, axis=-1, keepdims=True)


def _sb_kernel(q_ref, k_ref, v_ref, o_ref, carry_ref, acc_ref):
    tq = SB_BLOCK
    i = pl.program_id(2)
    npairs = q_ref.shape[1] // LANES
    suffix_mat = jnp.where(lax.broadcasted_iota(jnp.int32, (tq, tq), 0) > lax.broadcasted_iota(jnp.int32, (tq, tq), 1),
                           1.0, 0.0).astype(BF16)
    lo_lanes = lax.broadcasted_iota(jnp.int32, (tq, LANES), 1) < SB_HEAD_DIM

    def step(j, strict):
        start = pl.multiple_of(j * tq, tq)
        top = None
        for p in range(npairs):
            lanes = slice(p * LANES, (p + 1) * LANES)
            q = q_ref[:, lanes]
            kb = k_ref[pl.ds(start, tq), lanes]
            vb = v_ref[pl.ds(start, tq), lanes]
            zero = jnp.zeros_like(q)
            v_split = jnp.concatenate([jnp.where(lo_lanes, vb, zero), jnp.where(lo_lanes, zero, vb)], axis=0)
            ws = []
            for hh, qh in enumerate((jnp.where(lo_lanes, q, zero), jnp.where(lo_lanes, zero, q))):
                w, carry = _sb_head_block(qh, kb, carry_ref[2 * p + hh], suffix_mat, strict)
                carry_ref[2 * p + hh] = carry
                ws.append(w)
                top = carry if top is None else jnp.maximum(top, carry)
            acc_ref[p] += jnp.dot(jnp.concatenate(ws, axis=1), v_split, preferred_element_type=F32)
        return jnp.max(top)

    carry_ref[...] = jnp.zeros_like(carry_ref)
    acc_ref[...] = jnp.zeros_like(acc_ref)
    qpos = lax.broadcasted_iota(jnp.int32, (tq, tq), 0)
    kpos = lax.broadcasted_iota(jnp.int32, (tq, tq), 1)
    strict = kpos < qpos

    def first_two():
        step(i, strict)
        return step(i - 1, None)

    top = lax.cond(i > 0, first_two, lambda: step(i, strict))

    def live(state):
        j, top = state
        return jnp.logical_and(j >= 0, top > -SB_UNDERFLOW_LOG2)

    def body(state):
        j, _ = state
        return j - 1, step(j, None)

    lax.while_loop(live, body, (i - 2, top))
    for p in range(npairs):
        o_ref[:, p * LANES:(p + 1) * LANES] = acc_ref[p].astype(o_ref.dtype)


def _sb_attention(qkv, batch, seq):
    tq = SB_BLOCK
    nq = seq // tq
    width = SB_PAIRS_PER_STEP * LANES
    nsteps = D_MODEL // width
    return pl.pallas_call(
        _sb_kernel,
        grid=(batch, nsteps, nq),
        in_specs=[pl.BlockSpec((tq, width), lambda b, p, i: (b * nq + i, p)),
                  pl.BlockSpec((seq, width), lambda b, p, i: (b, nsteps + p)),
                  pl.BlockSpec((seq, width), lambda b, p, i: (b, 2 * nsteps + p))],
        out_specs=pl.BlockSpec((tq, width), lambda b, p, i: (b * nq + i, p)),
        out_shape=jax.ShapeDtypeStruct((batch * seq, D_MODEL), BF16),
        scratch_shapes=[pltpu.VMEM((2 * SB_PAIRS_PER_STEP, tq, LANES), F32),
                        pltpu.VMEM((SB_PAIRS_PER_STEP, tq, LANES), F32)],
        compiler_params=_params("arbitrary", "arbitrary", "arbitrary"),
        name="sb_attention",
    )(qkv, qkv, qkv)


def kernel(x, mix_norm, ffn_norm, final_norm, ssd_w_in, ssd_conv_w, ssd_conv_b, ssd_dt_bias, ssd_a_log, ssd_d, ssd_norm, ssd_w_out, sb_w_qkv, sb_w_out, ffn_w_in, ffn_conv_w, ffn_conv_b, ffn_w_out):
    batch, seq, d = x.shape
    xf = x.reshape(batch * seq, d)
    h = _rmsnorm(xf, mix_norm[0])
    out = None
    for i in range(DEPTH):
        j = i // 2
        if i % 2 == 0:
            w_dt = jnp.pad(ssd_w_in[j, :, SSD_D_INNER + SSD_CONV_DIM:], ((0, 0), (0, LANES - SSD_HEADS)))[None]
            z = _mm(h, ssd_w_in, j, 0, SSD_D_INNER, BF16, 2048, 1024, "ssd_z")
            xbc = _mm_conv_act(h, ssd_w_in, j, SSD_D_INNER, ssd_conv_w[j], ssd_conv_b[j], 1, seq, 1024, 1024, "ssd_xbc")
            dt_raw = _mm(h, w_dt, 0, 0, LANES, F32, 2048, LANES, "ssd_dt")
            mixed = _ssd_scan(z, xbc, dt_raw, ssd_dt_bias[j], ssd_a_log[j], ssd_d[j], ssd_norm[j], batch, seq)
            w_out = ssd_w_out
        else:
            qkv = _mm(h, sb_w_qkv, j, 0, 3 * d, BF16, 2048, 1024, "sb_qkv",
                      scaled_tiles=d // 1024, scale=LOG2E * SB_HEAD_DIM ** -0.5)
            mixed = _sb_attention(qkv, batch, seq)
            w_out = sb_w_out
        xf, h = _mm_res_norm(mixed, w_out, j, xf, ffn_norm[i], False, 1024, "mixer_out")
        act = _mm_conv_act(h, ffn_w_in, i, 0, ffn_conv_w[i], ffn_conv_b[i], 2, seq, 512, FFN_D_FF // 2, "ffn_in")
        if i + 1 < DEPTH:
            xf, h = _mm_res_norm(act, ffn_w_out, i, xf, mix_norm[i + 1], False, 1024, "ffn_out")
        else:
            out = _mm_res_norm(act, ffn_w_out, i, xf, final_norm, True, 1024, "ffn_out_final")
    return out.reshape(batch, seq, d)
```

```python
import functools
import math

import jax
import jax.numpy as jnp
from jax import lax
from jax.experimental import pallas as pl
from jax.experimental.pallas import tpu as pltpu

F32 = jnp.float32
BF16 = jnp.bfloat16

D_MODEL = 1024
DEPTH = 4
NORM_EPS = 1e-6
LOG2E = math.log2(math.e)

SSD_D_INNER = 2048
SSD_HEAD_DIM = 64
SSD_HEADS = 32
SSD_GROUPS = 8
SSD_HEADS_PER_GROUP = 4
SSD_STATE = 128
SSD_CHUNK = 128
SSD_CHUNKS_PER_STEP = 4
SSD_CONV_DIM = SSD_D_INNER + 2 * SSD_GROUPS * SSD_STATE
SSD_GROUP_WIDTH = SSD_HEADS_PER_GROUP * SSD_HEAD_DIM

SB_HEADS = 16
SB_HEAD_DIM = 64
SB_BLOCK = 256
SB_PAIRS_PER_STEP = 4
SB_UNDERFLOW_LOG2 = 160.0

FFN_D_FF = 2816

LANES = 128
SUBLANES = 8
VMEM_LIMIT_BYTES = 56 * 1024 * 1024


def _params(*semantics):
    return pltpu.CompilerParams(dimension_semantics=semantics, vmem_limit_bytes=VMEM_LIMIT_BYTES)


def _rms_scale(v):
    return lax.rsqrt(jnp.mean(v * v, axis=-1, keepdims=True) + NORM_EPS)


def _rmsnorm_kernel(x_ref, g_ref, o_ref):
    x = x_ref[...]
    o_ref[...] = (x * _rms_scale(x) * g_ref[...]).astype(o_ref.dtype)


def _rmsnorm(x, g, tm=1024):
    t, d = x.shape
    return pl.pallas_call(
        _rmsnorm_kernel,
        grid=(t // tm,),
        in_specs=[pl.BlockSpec((tm, d), lambda i: (i, 0)), pl.BlockSpec((1, d), lambda i: (0, 0))],
        out_specs=pl.BlockSpec((tm, d), lambda i: (i, 0)),
        out_shape=jax.ShapeDtypeStruct((t, d), BF16),
        compiler_params=_params("arbitrary"),
        name="rmsnorm",
    )(x, g.reshape(1, d))


def _mm_kernel(h_ref, w_ref, o_ref, wb_ref, *, scaled_tiles, scale):
    @pl.when(pl.program_id(1) == 0)
    def _():
        w = w_ref[...]
        if scaled_tiles:
            w = w * jnp.where(pl.program_id(0) < scaled_tiles, scale, 1.0)
        wb_ref[...] = w.astype(BF16)

    o_ref[...] = jnp.dot(h_ref[...], wb_ref[...], preferred_element_type=F32).astype(o_ref.dtype)


def _mm(h, w, layer, col0, n, out_dtype, tm, tn, name, scaled_tiles=0, scale=1.0):
    t, k = h.shape
    off = col0 // tn
    return pl.pallas_call(
        functools.partial(_mm_kernel, scaled_tiles=scaled_tiles, scale=scale),
        grid=(n // tn, t // tm),
        in_specs=[pl.BlockSpec((tm, k), lambda j, i: (i, 0)),
                  pl.BlockSpec((None, k, tn), lambda j, i: (layer, 0, j + off), pipeline_mode=pl.Buffered(1))],
        out_specs=pl.BlockSpec((tm, tn), lambda j, i: (i, j)),
        out_shape=jax.ShapeDtypeStruct((t, n), out_dtype),
        scratch_shapes=[pltpu.VMEM((k, tn), BF16)],
        compiler_params=_params("arbitrary", "arbitrary"),
        name=name,
    )(h, w)


def _causal_conv(u3, prev, cw, cb):
    width = cw.shape[0]
    n = u3.shape[2]
    before = jnp.concatenate([prev, u3[:-1]], axis=0)
    sub = lax.broadcasted_iota(jnp.int32, (1, SUBLANES, n), 1)
    y = cb.reshape(1, 1, n) + cw[width - 1:width, :].reshape(1, 1, n) * u3
    for j in range(1, width):
        mixed = jnp.where(sub >= SUBLANES - j, before, u3)
        y = y + cw[width - 1 - j:width - j, :].reshape(1, 1, n) * pltpu.roll(mixed, j, axis=1)
    return y


def _silu(y):
    return y * (1.0 / (1.0 + jnp.exp2(y * -LOG2E)))


def _mm_conv_act_kernel(*refs, nbranch, tiles_per_seq):
    h_ref = refs[0]
    w_refs = refs[1:1 + nbranch]
    cw_refs = refs[1 + nbranch:1 + 2 * nbranch]
    cb_refs = refs[1 + 2 * nbranch:1 + 3 * nbranch]
    o_ref = refs[1 + 3 * nbranch]
    prev_refs = refs[2 + 3 * nbranch:2 + 4 * nbranch]
    wb_ref = refs[2 + 4 * nbranch]
    tm = h_ref.shape[0]
    tn = o_ref.shape[1]

    @pl.when(pl.program_id(1) == 0)
    def _():
        for b, w_ref in enumerate(w_refs):
            wb_ref[:, b * tn:(b + 1) * tn] = w_ref[...].astype(BF16)

    @pl.when(pl.program_id(1) % tiles_per_seq == 0)
    def _():
        for prev_ref in prev_refs:
            prev_ref[...] = jnp.zeros_like(prev_ref)

    u_all = jnp.dot(h_ref[...], wb_ref[...], preferred_element_type=F32)
    ys = []
    for b in range(nbranch):
        u = u_all[:, b * tn:(b + 1) * tn]
        u3 = u.reshape(tm // SUBLANES, SUBLANES, u.shape[1])
        ys.append(_causal_conv(u3, prev_refs[b][...], cw_refs[b][...], cb_refs[b][...]))
        prev_refs[b][...] = u3[tm // SUBLANES - 1:]
    act = _silu(ys[0])
    if nbranch == 2:
        act = act * ys[1]
    o_ref[...] = act.reshape(o_ref.shape).astype(o_ref.dtype)


def _mm_conv_act(h, w, layer, col0, cw, cb, nbranch, seq, tm, tn, name):
    t, k = h.shape
    n_out = cw.shape[1] // nbranch
    width = cw.shape[0]
    ncol = n_out // tn
    off = col0 // tn
    in_specs = [pl.BlockSpec((tm, k), lambda j, i: (i, 0))]
    in_specs += [pl.BlockSpec((None, k, tn), lambda j, i, b=b: (layer, 0, j + b * ncol + off),
                              pipeline_mode=pl.Buffered(1)) for b in range(nbranch)]
    in_specs += [pl.BlockSpec((width, tn), lambda j, i, b=b: (0, j + b * ncol)) for b in range(nbranch)]
    in_specs += [pl.BlockSpec((1, tn), lambda j, i, b=b: (0, j + b * ncol)) for b in range(nbranch)]
    cb2 = cb.reshape(1, -1)
    return pl.pallas_call(
        functools.partial(_mm_conv_act_kernel, nbranch=nbranch, tiles_per_seq=seq // tm),
        grid=(ncol, t // tm),
        in_specs=in_specs,
        out_specs=pl.BlockSpec((tm, tn), lambda j, i: (i, j)),
        out_shape=jax.ShapeDtypeStruct((t, n_out), BF16),
        scratch_shapes=[pltpu.VMEM((1, SUBLANES, tn), F32)] * nbranch + [pltpu.VMEM((k, nbranch * tn), BF16)],
        compiler_params=_params("arbitrary", "arbitrary"),
        name=name,
    )(h, *([w] * nbranch), *([cw] * nbranch), *([cb2] * nbranch))


def _residual_update(a_ref, w_ref, x_ref, wb_ref):
    @pl.when(pl.program_id(0) == 0)
    def _():
        wb_ref[...] = w_ref[...].astype(BF16)

    return x_ref[...] + jnp.dot(a_ref[...], wb_ref[...], preferred_element_type=F32)


def _mm_res_norm_kernel(a_ref, w_ref, x_ref, g_ref, xo_ref, ho_ref, wb_ref):
    xn = _residual_update(a_ref, w_ref, x_ref, wb_ref)
    xo_ref[...] = xn
    ho_ref[...] = (xn * _rms_scale(xn) * g_ref[...]).astype(ho_ref.dtype)


def _mm_res_final_kernel(a_ref, w_ref, x_ref, g_ref, o_ref, wb_ref):
    xn = _residual_update(a_ref, w_ref, x_ref, wb_ref)
    o_ref[...] = xn * _rms_scale(xn) * g_ref[...]


def _mm_res_norm(a, w, layer, x, g, final, tm, name):
    t, k = a.shape
    d = w.shape[2]
    row = pl.BlockSpec((tm, d), lambda i: (i, 0))
    in_specs = [pl.BlockSpec((tm, k), lambda i: (i, 0)),
                pl.BlockSpec((None, k, d), lambda i: (layer, 0, 0), pipeline_mode=pl.Buffered(1)),
                row, pl.BlockSpec((1, d), lambda i: (0, 0))]
    scratch = [pltpu.VMEM((k, d), BF16)]
    if final:
        return pl.pallas_call(
            _mm_res_final_kernel, grid=(t // tm,), in_specs=in_specs, out_specs=row,
            out_shape=jax.ShapeDtypeStruct((t, d), F32), scratch_shapes=scratch,
            compiler_params=_params("arbitrary"), name=name,
        )(a, w, x, g.reshape(1, d))
    return pl.pallas_call(
        _mm_res_norm_kernel, grid=(t // tm,), in_specs=in_specs, out_specs=[row, row],
        out_shape=[jax.ShapeDtypeStruct((t, d), F32), jax.ShapeDtypeStruct((t, d), BF16)], scratch_shapes=scratch,
        compiler_params=_params("arbitrary"), name=name,
    )(a, w, x, g.reshape(1, d))


def _split3(v):
    a = v.astype(BF16)
    r = v - a.astype(F32)
    b = r.astype(BF16)
    c = (r - b.astype(F32)).astype(BF16)
    return a, b, c


def _expand_heads(v, h0, nheads):
    rows = v.shape[0]
    lane = lax.broadcasted_iota(jnp.int32, (rows, LANES), 1)
    pairs = []
    for p in range(nheads // 2):
        if rows % SUBLANES:
            lo = jnp.broadcast_to(v[:, h0 + 2 * p:h0 + 2 * p + 1], (rows, LANES))
            hi = jnp.broadcast_to(v[:, h0 + 2 * p + 1:h0 + 2 * p + 2], (rows, LANES))
            pairs.append(jnp.where(lane < SSD_HEAD_DIM, lo, hi))
        else:
            pairs.append(jnp.take_along_axis(v, h0 + 2 * p + lane // SSD_HEAD_DIM, axis=1))
    return jnp.concatenate(pairs, axis=1)


def _ssd_kernel(z_ref, xbc_ref, dt_ref, dtb_ref, alog_ref, dskip_ref, ng_ref, o_ref, state_ref, y_ref):
    @pl.when(pl.program_id(1) == 0)
    def _():
        state_ref[...] = jnp.zeros_like(state_ref)

    for c in range(SSD_CHUNKS_PER_STEP):
        _ssd_chunk(slice(c * SSD_CHUNK, (c + 1) * SSD_CHUNK), z_ref, xbc_ref, dt_ref, dtb_ref, alog_ref, dskip_ref,
                   ng_ref, o_ref, state_ref, y_ref)


def _ssd_chunk(rows, z_ref, xbc_ref, dt_ref, dtb_ref, alog_ref, dskip_ref, ng_ref, o_ref, state_ref, y_ref):
    L = SSD_CHUNK
    pre = dt_ref[rows, :] + dtb_ref[...]
    dt = jnp.maximum(pre, 0.0) + jnp.log1p(jnp.exp(-jnp.abs(pre)))
    a = (-LOG2E * jnp.exp(alog_ref[...])) * dt
    row = lax.broadcasted_iota(jnp.int32, (L, L), 0)
    col = lax.broadcasted_iota(jnp.int32, (L, L), 1)
    causal = row >= col
    tri = jnp.where(causal, 1.0, 0.0).astype(BF16)
    a1, a2, a3 = _split3(a)
    acum = (jnp.dot(tri, a1, preferred_element_type=F32) + jnp.dot(tri, a2, preferred_element_type=F32)
            + jnp.dot(tri, a3, preferred_element_type=F32))
    src_t = (acum - jnp.log2(jnp.maximum(dt, jnp.finfo(F32).tiny))).T
    last = acum[L - 1:L, :]
    exp_acum = jnp.exp2(acum)
    w_dt = jnp.exp2(last - acum) * dt
    exp_last = jnp.exp2(last)
    lane = lax.broadcasted_iota(jnp.int32, (L, LANES), 1)
    lo_lanes = lane < SSD_HEAD_DIM

    for g in range(SSD_GROUPS):
        b_g = xbc_ref[rows, SSD_D_INNER + g * SSD_STATE:SSD_D_INNER + (g + 1) * SSD_STATE]
        c_off = SSD_D_INNER + SSD_GROUPS * SSD_STATE
        c_g = xbc_ref[rows, c_off + g * SSD_STATE:c_off + (g + 1) * SSD_STATE]
        cb = lax.dot_general(c_g, b_g, (((1,), (1,)), ((), ())), preferred_element_type=F32)
        x0 = g * SSD_GROUP_WIDTH
        h0 = g * SSD_HEADS_PER_GROUP
        y_diag = []
        for p in range(SSD_HEADS_PER_GROUP // 2):
            scores = []
            for hh in (h0 + 2 * p, h0 + 2 * p + 1):
                seg = acum[:, hh:hh + 1] - src_t[hh:hh + 1, :]
                scores.append((cb * jnp.exp2(jnp.where(causal, seg, -jnp.inf))).astype(BF16))
            xs = xbc_ref[rows, x0 + p * LANES:x0 + (p + 1) * LANES]
            zero = jnp.zeros_like(xs)
            xs_split = jnp.concatenate([jnp.where(lo_lanes, xs, zero), jnp.where(lo_lanes, zero, xs)], axis=0)
            y_diag.append(jnp.dot(jnp.concatenate(scores, axis=1), xs_split, preferred_element_type=F32))
        y_diag = jnp.concatenate(y_diag, axis=1)

        xs_g = xbc_ref[rows, x0:x0 + SSD_GROUP_WIDTH].astype(F32)
        state = state_ref[g]
        y_off = (jnp.dot(c_g, state.astype(BF16), preferred_element_type=F32)
                 * _expand_heads(exp_acum, h0, SSD_HEADS_PER_GROUP))
        xw = (xs_g * _expand_heads(w_dt, h0, SSD_HEADS_PER_GROUP)).astype(BF16)
        state_ref[g] = (state * _expand_heads(exp_last, h0, SSD_HEADS_PER_GROUP)
                        + lax.dot_general(b_g, xw, (((0,), (0,)), ((), ())), preferred_element_type=F32))
        y_ref[rows, x0:x0 + SSD_GROUP_WIDTH] = y_diag + y_off + dskip_ref[:, x0:x0 + SSD_GROUP_WIDTH] * xs_g

    z = z_ref[rows, :].astype(F32)
    gated = y_ref[rows, :] * _silu(z)
    o_ref[rows, :] = (gated * _rms_scale(gated) * ng_ref[...]).astype(o_ref.dtype)


def _ssd_scan(z, xbc, dt_raw, dt_bias, a_log, d_skip, norm_g, batch, seq):
    L = SSD_CHUNKS_PER_STEP * SSD_CHUNK
    nc = seq // L
    pad = LANES - SSD_HEADS
    dtb = jnp.pad(dt_bias, (0, pad)).reshape(1, LANES)
    alog = jnp.pad(a_log, (0, pad)).reshape(1, LANES)
    dskip = jnp.repeat(d_skip, SSD_HEAD_DIM).reshape(1, SSD_D_INNER)
    small = lambda n: pl.BlockSpec((1, n), lambda b, c: (0, 0))
    return pl.pallas_call(
        _ssd_kernel,
        grid=(batch, nc),
        in_specs=[pl.BlockSpec((L, SSD_D_INNER), lambda b, c: (b * nc + c, 0)),
                  pl.BlockSpec((L, SSD_CONV_DIM), lambda b, c: (b * nc + c, 0)),
                  pl.BlockSpec((L, LANES), lambda b, c: (b * nc + c, 0)),
                  small(LANES), small(LANES), small(SSD_D_INNER), small(SSD_D_INNER)],
        out_specs=pl.BlockSpec((L, SSD_D_INNER), lambda b, c: (b * nc + c, 0)),
        out_shape=jax.ShapeDtypeStruct((batch * seq, SSD_D_INNER), BF16),
        scratch_shapes=[pltpu.VMEM((SSD_GROUPS, SSD_STATE, SSD_GROUP_WIDTH), F32),
                        pltpu.VMEM((L, SSD_D_INNER), F32)],
        compiler_params=_params("arbitrary", "arbitrary"),
        name="ssd_scan",
    )(z, xbc, dt_raw, dtb, alog, dskip, norm_g.reshape(1, SSD_D_INNER))


def _sb_head_block(qh, kb, carry, suffix_mat, strict):
    tk = kb.shape[0]
    l = lax.dot_general(qh, kb, (((1,), (1,)), ((), ())), preferred_element_type=F32)
    neg_abs = lax.bitcast_convert_type(lax.bitcast_convert_type(l, jnp.uint32) | jnp.uint32(0x80000000), F32)
    lg = jnp.log2(1.0 + jnp.exp2(neg_abs))
    log_beta = jnp.minimum(l, 0.0) - lg
    log_fail = log_beta - l
    if strict is not None:
        log_fail = jnp.where(strict, log_fail, 0.0)
    inner = jnp.dot(log_fail.astype(BF16), suffix_mat, preferred_element_type=F32)
    w = jnp.exp2(log_beta + inner + jnp.concatenate([carry] * (tk // LANES), axis=1))
    if strict is not None:
        w = jnp.where(strict, w, 0.0)
    return w.astype(BF16), carry + jnp.sum(log_fail, axis=-1, keepdims=True)


def _sb_kernel(q_ref, k_ref, v_ref, o_ref, carry_ref, acc_ref):
    tq = SB_BLOCK
    i = pl.program_id(2)
    npairs = q_ref.shape[1] // LANES
    suffix_mat = jnp.where(lax.broadcasted_iota(jnp.int32, (tq, tq), 0) > lax.broadcasted_iota(jnp.int32, (tq, tq), 1),
                           1.0, 0.0).astype(BF16)
    lo_lanes = lax.broadcasted_iota(jnp.int32, (tq, LANES), 1) < SB_HEAD_DIM

    def pair_step(p, j, strict):
        start = pl.multiple_of(j * tq, tq)
        lanes = slice(p * LANES, (p + 1) * LANES)
        q = q_ref[:, lanes]
        kb = k_ref[pl.ds(start, tq), lanes]
        vb = v_ref[pl.ds(start, tq), lanes]
        zero = jnp.zeros_like(q)
        v_split = jnp.concatenate([jnp.where(lo_lanes, vb, zero), jnp.where(lo_lanes, zero, vb)], axis=0)
        ws, carries = [], []
        for hh, qh in enumerate((jnp.where(lo_lanes, q, zero), jnp.where(lo_lanes, zero, q))):
            w, carry = _sb_head_block(qh, kb, carry_ref[2 * p + hh], suffix_mat, strict)
            carry_ref[2 * p + hh] = carry
            ws.append(w)
            carries.append(carry)
        acc_ref[p] += jnp.dot(jnp.concatenate(ws, axis=1), v_split, preferred_element_type=F32)
        return jnp.max(jnp.maximum(*carries))

    def step(j, strict):
        return tuple(pair_step(p, j, strict) for p in range(npairs))

    carry_ref[...] = jnp.zeros_like(carry_ref)
    acc_ref[...] = jnp.zeros_like(acc_ref)
    qpos = lax.broadcasted_iota(jnp.int32, (tq, tq), 0)
    kpos = lax.broadcasted_iota(jnp.int32, (tq, tq), 1)
    strict = kpos < qpos

    def first_two():
        step(i, strict)
        return step(i - 1, None)

    tops = lax.cond(i > 0, first_two, lambda: step(i, strict))

    def alive(top):
        return top > -SB_UNDERFLOW_LOG2

    def live(state):
        j, tops = state
        return jnp.logical_and(j >= 0, alive(functools.reduce(jnp.maximum, tops)))

    def body(state):
        j, tops = state
        return j - 1, tuple(lax.cond(alive(top), functools.partial(pair_step, p, j, None), lambda top=top: top)
                            for p, top in enumerate(tops))

    lax.while_loop(live, body, (i - 2, tops))
    for p in range(npairs):
        o_ref[:, p * LANES:(p + 1) * LANES] = acc_ref[p].astype(o_ref.dtype)


def _sb_attention(qkv, batch, seq):
    tq = SB_BLOCK
    nq = seq // tq
    width = SB_PAIRS_PER_STEP * LANES
    nsteps = D_MODEL // width
    return pl.pallas_call(
        _sb_kernel,
        grid=(batch, nsteps, nq),
        in_specs=[pl.BlockSpec((tq, width), lambda b, p, i: (b * nq + i, p)),
                  pl.BlockSpec((seq, width), lambda b, p, i: (b, nsteps + p)),
                  pl.BlockSpec((seq, width), lambda b, p, i: (b, 2 * nsteps + p))],
        out_specs=pl.BlockSpec((tq, width), lambda b, p, i: (b * nq + i, p)),
        out_shape=jax.ShapeDtypeStruct((batch * seq, D_MODEL), BF16),
        scratch_shapes=[pltpu.VMEM((2 * SB_PAIRS_PER_STEP, tq, LANES), F32),
                        pltpu.VMEM((SB_PAIRS_PER_STEP, tq, LANES), F32)],
        compiler_params=_params("arbitrary", "arbitrary", "arbitrary"),
        name="sb_attention",
    )(qkv, qkv, qkv)


def kernel(x, mix_norm, ffn_norm, final_norm, ssd_w_in, ssd_conv_w, ssd_conv_b, ssd_dt_bias, ssd_a_log, ssd_d, ssd_norm, ssd_w_out, sb_w_qkv, sb_w_out, ffn_w_in, ffn_conv_w, ffn_conv_b, ffn_w_out):
    batch, seq, d = x.shape
    xf = x.reshape(batch * seq, d)
    h = _rmsnorm(xf, mix_norm[0])
    out = None
    for i in range(DEPTH):
        j = i // 2
        if i % 2 == 0:
            w_dt = jnp.pad(ssd_w_in[j, :, SSD_D_INNER + SSD_CONV_DIM:], ((0, 0), (0, LANES - SSD_HEADS)))[None]
            z = _mm(h, ssd_w_in, j, 0, SSD_D_INNER, BF16, 2048, 1024, "ssd_z")
            xbc = _mm_conv_act(h, ssd_w_in, j, SSD_D_INNER, ssd_conv_w[j], ssd_conv_b[j], 1, seq, 1024, 1024, "ssd_xbc")
            dt_raw = _mm(h, w_dt, 0, 0, LANES, F32, 2048, LANES, "ssd_dt")
            mixed = _ssd_scan(z, xbc, dt_raw, ssd_dt_bias[j], ssd_a_log[j], ssd_d[j], ssd_norm[j], batch, seq)
            w_out = ssd_w_out
        else:
            qkv = _mm(h, sb_w_qkv, j, 0, 3 * d, BF16, 2048, 1024, "sb_qkv",
                      scaled_tiles=d // 1024, scale=LOG2E * SB_HEAD_DIM ** -0.5)
            mixed = _sb_attention(qkv, batch, seq)
            w_out = sb_w_out
        xf, h = _mm_res_norm(mixed, w_out, j, xf, ffn_norm[i], False, 1024, "mixer_out")
        act = _mm_conv_act(h, ffn_w_in, i, 0, ffn_conv_w[i], ffn_conv_b[i], 2, seq, 512, FFN_D_FF // 2, "ffn_in")
        if i + 1 < DEPTH:
            xf, h = _mm_res_norm(act, ffn_w_out, i, xf, mix_norm[i + 1], False, 1024, "ffn_out")
        else:
            out = _mm_res_norm(act, ffn_w_out, i, xf, final_norm, True, 1024, "ffn_out_final")
    return out.reshape(batch, seq, d)
```

```python
import functools
import math

import jax
import jax.numpy as jnp
from jax import lax
from jax.experimental import pallas as pl
from jax.experimental.pallas import tpu as pltpu

F32 = jnp.float32
BF16 = jnp.bfloat16

D_MODEL = 1024
DEPTH = 4
NORM_EPS = 1e-6
LOG2E = math.log2(math.e)

SSD_D_INNER = 2048
SSD_HEAD_DIM = 64
SSD_HEADS = 32
SSD_GROUPS = 8
SSD_HEADS_PER_GROUP = 4
SSD_STATE = 128
SSD_CHUNK = 128
SSD_CHUNKS_PER_STEP = 4
SSD_CONV_DIM = SSD_D_INNER + 2 * SSD_GROUPS * SSD_STATE
SSD_GROUP_WIDTH = SSD_HEADS_PER_GROUP * SSD_HEAD_DIM

SB_HEADS = 16
SB_HEAD_DIM = 64
SB_BLOCK = 256
SB_PAIRS_PER_STEP = 4
SB_UNDERFLOW_LOG2 = 160.0
SB_NEIGHBOUR_ROWS = 192

FFN_D_FF = 2816

LANES = 128
SUBLANES = 8
VMEM_LIMIT_BYTES = 56 * 1024 * 1024


def _params(*semantics):
    return pltpu.CompilerParams(dimension_semantics=semantics, vmem_limit_bytes=VMEM_LIMIT_BYTES)


def _rms_scale(v):
    return lax.rsqrt(jnp.mean(v * v, axis=-1, keepdims=True) + NORM_EPS)


def _rmsnorm_kernel(x_ref, g_ref, o_ref):
    x = x_ref[...]
    o_ref[...] = (x * _rms_scale(x) * g_ref[...]).astype(o_ref.dtype)


def _rmsnorm(x, g, tm=1024):
    t, d = x.shape
    return pl.pallas_call(
        _rmsnorm_kernel,
        grid=(t // tm,),
        in_specs=[pl.BlockSpec((tm, d), lambda i: (i, 0)), pl.BlockSpec((1, d), lambda i: (0, 0))],
        out_specs=pl.BlockSpec((tm, d), lambda i: (i, 0)),
        out_shape=jax.ShapeDtypeStruct((t, d), BF16),
        compiler_params=_params("arbitrary"),
        name="rmsnorm",
    )(x, g.reshape(1, d))


def _mm_kernel(h_ref, w_ref, o_ref, wb_ref, *, scaled_tiles, scale):
    @pl.when(pl.program_id(1) == 0)
    def _():
        w = w_ref[...]
        if scaled_tiles:
            w = w * jnp.where(pl.program_id(0) < scaled_tiles, scale, 1.0)
        wb_ref[...] = w.astype(BF16)

    o_ref[...] = jnp.dot(h_ref[...], wb_ref[...], preferred_element_type=F32).astype(o_ref.dtype)


def _mm(h, w, layer, col0, n, out_dtype, tm, tn, name, scaled_tiles=0, scale=1.0):
    t, k = h.shape
    off = col0 // tn
    return pl.pallas_call(
        functools.partial(_mm_kernel, scaled_tiles=scaled_tiles, scale=scale),
        grid=(n // tn, t // tm),
        in_specs=[pl.BlockSpec((tm, k), lambda j, i: (i, 0)),
                  pl.BlockSpec((None, k, tn), lambda j, i: (layer, 0, j + off), pipeline_mode=pl.Buffered(1))],
        out_specs=pl.BlockSpec((tm, tn), lambda j, i: (i, j)),
        out_shape=jax.ShapeDtypeStruct((t, n), out_dtype),
        scratch_shapes=[pltpu.VMEM((k, tn), BF16)],
        compiler_params=_params("arbitrary", "arbitrary"),
        name=name,
    )(h, w)


def _causal_conv(u3, prev, cw, cb):
    width = cw.shape[0]
    n = u3.shape[2]
    before = jnp.concatenate([prev, u3[:-1]], axis=0)
    sub = lax.broadcasted_iota(jnp.int32, (1, SUBLANES, n), 1)
    y = cb.reshape(1, 1, n) + cw[width - 1:width, :].reshape(1, 1, n) * u3
    for j in range(1, width):
        mixed = jnp.where(sub >= SUBLANES - j, before, u3)
        y = y + cw[width - 1 - j:width - j, :].reshape(1, 1, n) * pltpu.roll(mixed, j, axis=1)
    return y


def _silu(y):
    return y * (1.0 / (1.0 + jnp.exp2(y * -LOG2E)))


def _mm_conv_act_kernel(*refs, nbranch, tiles_per_seq):
    h_ref = refs[0]
    w_refs = refs[1:1 + nbranch]
    cw_refs = refs[1 + nbranch:1 + 2 * nbranch]
    cb_refs = refs[1 + 2 * nbranch:1 + 3 * nbranch]
    o_ref = refs[1 + 3 * nbranch]
    prev_refs = refs[2 + 3 * nbranch:2 + 4 * nbranch]
    wb_ref = refs[2 + 4 * nbranch]
    tm = h_ref.shape[0]
    tn = o_ref.shape[1]

    @pl.when(pl.program_id(1) == 0)
    def _():
        for b, w_ref in enumerate(w_refs):
            wb_ref[:, b * tn:(b + 1) * tn] = w_ref[...].astype(BF16)

    @pl.when(pl.program_id(1) % tiles_per_seq == 0)
    def _():
        for prev_ref in prev_refs:
            prev_ref[...] = jnp.zeros_like(prev_ref)

    u_all = jnp.dot(h_ref[...], wb_ref[...], preferred_element_type=F32)
    ys = []
    for b in range(nbranch):
        u = u_all[:, b * tn:(b + 1) * tn]
        u3 = u.reshape(tm // SUBLANES, SUBLANES, u.shape[1])
        ys.append(_causal_conv(u3, prev_refs[b][...], cw_refs[b][...], cb_refs[b][...]))
        prev_refs[b][...] = u3[tm // SUBLANES - 1:]
    act = _silu(ys[0])
    if nbranch == 2:
        act = act * ys[1]
    o_ref[...] = act.reshape(o_ref.shape).astype(o_ref.dtype)


def _mm_conv_act(h, w, layer, col0, cw, cb, nbranch, seq, tm, tn, name):
    t, k = h.shape
    n_out = cw.shape[1] // nbranch
    width = cw.shape[0]
    ncol = n_out // tn
    off = col0 // tn
    in_specs = [pl.BlockSpec((tm, k), lambda j, i: (i, 0))]
    in_specs += [pl.BlockSpec((None, k, tn), lambda j, i, b=b: (layer, 0, j + b * ncol + off),
                              pipeline_mode=pl.Buffered(1)) for b in range(nbranch)]
    in_specs += [pl.BlockSpec((width, tn), lambda j, i, b=b: (0, j + b * ncol)) for b in range(nbranch)]
    in_specs += [pl.BlockSpec((1, tn), lambda j, i, b=b: (0, j + b * ncol)) for b in range(nbranch)]
    cb2 = cb.reshape(1, -1)
    return pl.pallas_call(
        functools.partial(_mm_conv_act_kernel, nbranch=nbranch, tiles_per_seq=seq // tm),
        grid=(ncol, t // tm),
        in_specs=in_specs,
        out_specs=pl.BlockSpec((tm, tn), lambda j, i: (i, j)),
        out_shape=jax.ShapeDtypeStruct((t, n_out), BF16),
        scratch_shapes=[pltpu.VMEM((1, SUBLANES, tn), F32)] * nbranch + [pltpu.VMEM((k, nbranch * tn), BF16)],
        compiler_params=_params("arbitrary", "arbitrary"),
        name=name,
    )(h, *([w] * nbranch), *([cw] * nbranch), *([cb2] * nbranch))


def _residual_update(a_ref, w_ref, x_ref, wb_ref):
    @pl.when(pl.program_id(0) == 0)
    def _():
        wb_ref[...] = w_ref[...].astype(BF16)

    return x_ref[...] + jnp.dot(a_ref[...], wb_ref[...], preferred_element_type=F32)


def _mm_res_norm_kernel(a_ref, w_ref, x_ref, g_ref, xo_ref, ho_ref, wb_ref):
    xn = _residual_update(a_ref, w_ref, x_ref, wb_ref)
    xo_ref[...] = xn
    ho_ref[...] = (xn * _rms_scale(xn) * g_ref[...]).astype(ho_ref.dtype)


def _mm_res_final_kernel(a_ref, w_ref, x_ref, g_ref, o_ref, wb_ref):
    xn = _residual_update(a_ref, w_ref, x_ref, wb_ref)
    o_ref[...] = xn * _rms_scale(xn) * g_ref[...]


def _mm_res_norm(a, w, layer, x, g, final, tm, name):
    t, k = a.shape
    d = w.shape[2]
    row = pl.BlockSpec((tm, d), lambda i: (i, 0))
    in_specs = [pl.BlockSpec((tm, k), lambda i: (i, 0)),
                pl.BlockSpec((None, k, d), lambda i: (layer, 0, 0), pipeline_mode=pl.Buffered(1)),
                row, pl.BlockSpec((1, d), lambda i: (0, 0))]
    scratch = [pltpu.VMEM((k, d), BF16)]
    if final:
        return pl.pallas_call(
            _mm_res_final_kernel, grid=(t // tm,), in_specs=in_specs, out_specs=row,
            out_shape=jax.ShapeDtypeStruct((t, d), F32), scratch_shapes=scratch,
            compiler_params=_params("arbitrary"), name=name,
        )(a, w, x, g.reshape(1, d))
    return pl.pallas_call(
        _mm_res_norm_kernel, grid=(t // tm,), in_specs=in_specs, out_specs=[row, row],
        out_shape=[jax.ShapeDtypeStruct((t, d), F32), jax.ShapeDtypeStruct((t, d), BF16)], scratch_shapes=scratch,
        compiler_params=_params("arbitrary"), name=name,
    )(a, w, x, g.reshape(1, d))


def _split3(v):
    a = v.astype(BF16)
    r = v - a.astype(F32)
    b = r.astype(BF16)
    c = (r - b.astype(F32)).astype(BF16)
    return a, b, c


def _expand_heads(v, h0, nheads):
    rows = v.shape[0]
    lane = lax.broadcasted_iota(jnp.int32, (rows, LANES), 1)
    pairs = []
    for p in range(nheads // 2):
        if rows % SUBLANES:
            lo = jnp.broadcast_to(v[:, h0 + 2 * p:h0 + 2 * p + 1], (rows, LANES))
            hi = jnp.broadcast_to(v[:, h0 + 2 * p + 1:h0 + 2 * p + 2], (rows, LANES))
            pairs.append(jnp.where(lane < SSD_HEAD_DIM, lo, hi))
        else:
            pairs.append(jnp.take_along_axis(v, h0 + 2 * p + lane // SSD_HEAD_DIM, axis=1))
    return jnp.concatenate(pairs, axis=1)


def _ssd_kernel(z_ref, xbc_ref, dt_ref, dtb_ref, alog_ref, dskip_ref, ng_ref, o_ref, state_ref, y_ref):
    @pl.when(pl.program_id(1) == 0)
    def _():
        state_ref[...] = jnp.zeros_like(state_ref)

    for c in range(SSD_CHUNKS_PER_STEP):
        _ssd_chunk(slice(c * SSD_CHUNK, (c + 1) * SSD_CHUNK), z_ref, xbc_ref, dt_ref, dtb_ref, alog_ref, dskip_ref,
                   ng_ref, o_ref, state_ref, y_ref)


def _ssd_chunk(rows, z_ref, xbc_ref, dt_ref, dtb_ref, alog_ref, dskip_ref, ng_ref, o_ref, state_ref, y_ref):
    L = SSD_CHUNK
    pre = dt_ref[rows, :] + dtb_ref[...]
    dt = jnp.maximum(pre, 0.0) + jnp.log1p(jnp.exp(-jnp.abs(pre)))
    a = (-LOG2E * jnp.exp(alog_ref[...])) * dt
    row = lax.broadcasted_iota(jnp.int32, (L, L), 0)
    col = lax.broadcasted_iota(jnp.int32, (L, L), 1)
    causal = row >= col
    tri = jnp.where(causal, 1.0, 0.0).astype(BF16)
    a1, a2, a3 = _split3(a)
    acum = (jnp.dot(tri, a1, preferred_element_type=F32) + jnp.dot(tri, a2, preferred_element_type=F32)
            + jnp.dot(tri, a3, preferred_element_type=F32))
    src_t = (acum - jnp.log2(jnp.maximum(dt, jnp.finfo(F32).tiny))).T
    last = acum[L - 1:L, :]
    exp_acum = jnp.exp2(acum)
    w_dt = jnp.exp2(last - acum) * dt
    exp_last = jnp.exp2(last)
    lane = lax.broadcasted_iota(jnp.int32, (L, LANES), 1)
    lo_lanes = lane < SSD_HEAD_DIM

    for g in range(SSD_GROUPS):
        b_g = xbc_ref[rows, SSD_D_INNER + g * SSD_STATE:SSD_D_INNER + (g + 1) * SSD_STATE]
        c_off = SSD_D_INNER + SSD_GROUPS * SSD_STATE
        c_g = xbc_ref[rows, c_off + g * SSD_STATE:c_off + (g + 1) * SSD_STATE]
        cb = lax.dot_general(c_g, b_g, (((1,), (1,)), ((), ())), preferred_element_type=F32)
        x0 = g * SSD_GROUP_WIDTH
        h0 = g * SSD_HEADS_PER_GROUP
        y_diag = []
        for p in range(SSD_HEADS_PER_GROUP // 2):
            scores = []
            for hh in (h0 + 2 * p, h0 + 2 * p + 1):
                seg = acum[:, hh:hh + 1] - src_t[hh:hh + 1, :]
                scores.append((cb * jnp.exp2(jnp.where(causal, seg, -jnp.inf))).astype(BF16))
            xs = xbc_ref[rows, x0 + p * LANES:x0 + (p + 1) * LANES]
            zero = jnp.zeros_like(xs)
            xs_split = jnp.concatenate([jnp.where(lo_lanes, xs, zero), jnp.where(lo_lanes, zero, xs)], axis=0)
            y_diag.append(jnp.dot(jnp.concatenate(scores, axis=1), xs_split, preferred_element_type=F32))
        y_diag = jnp.concatenate(y_diag, axis=1)

        xs_g = xbc_ref[rows, x0:x0 + SSD_GROUP_WIDTH].astype(F32)
        state = state_ref[g]
        y_off = (jnp.dot(c_g, state.astype(BF16), preferred_element_type=F32)
                 * _expand_heads(exp_acum, h0, SSD_HEADS_PER_GROUP))
        xw = (xs_g * _expand_heads(w_dt, h0, SSD_HEADS_PER_GROUP)).astype(BF16)
        state_ref[g] = (state * _expand_heads(exp_last, h0, SSD_HEADS_PER_GROUP)
                        + lax.dot_general(b_g, xw, (((0,), (0,)), ((), ())), preferred_element_type=F32))
        y_ref[rows, x0:x0 + SSD_GROUP_WIDTH] = y_diag + y_off + dskip_ref[:, x0:x0 + SSD_GROUP_WIDTH] * xs_g

    z = z_ref[rows, :].astype(F32)
    gated = y_ref[rows, :] * _silu(z)
    o_ref[rows, :] = (gated * _rms_scale(gated) * ng_ref[...]).astype(o_ref.dtype)


def _ssd_scan(z, xbc, dt_raw, dt_bias, a_log, d_skip, norm_g, batch, seq):
    L = SSD_CHUNKS_PER_STEP * SSD_CHUNK
    nc = seq // L
    pad = LANES - SSD_HEADS
    dtb = jnp.pad(dt_bias, (0, pad)).reshape(1, LANES)
    alog = jnp.pad(a_log, (0, pad)).reshape(1, LANES)
    dskip = jnp.repeat(d_skip, SSD_HEAD_DIM).reshape(1, SSD_D_INNER)
    small = lambda n: pl.BlockSpec((1, n), lambda b, c: (0, 0))
    return pl.pallas_call(
        _ssd_kernel,
        grid=(batch, nc),
        in_specs=[pl.BlockSpec((L, SSD_D_INNER), lambda b, c: (b * nc + c, 0)),
                  pl.BlockSpec((L, SSD_CONV_DIM), lambda b, c: (b * nc + c, 0)),
                  pl.BlockSpec((L, LANES), lambda b, c: (b * nc + c, 0)),
                  small(LANES), small(LANES), small(SSD_D_INNER), small(SSD_D_INNER)],
        out_specs=pl.BlockSpec((L, SSD_D_INNER), lambda b, c: (b * nc + c, 0)),
        out_shape=jax.ShapeDtypeStruct((batch * seq, SSD_D_INNER), BF16),
        scratch_shapes=[pltpu.VMEM((SSD_GROUPS, SSD_STATE, SSD_GROUP_WIDTH), F32),
                        pltpu.VMEM((L, SSD_D_INNER), F32)],
        compiler_params=_params("arbitrary", "arbitrary"),
        name="ssd_scan",
    )(z, xbc, dt_raw, dtb, alog, dskip, norm_g.reshape(1, SSD_D_INNER))


def _sb_head_block(qh, kb, carry, suffix_mat, strict):
    tk = kb.shape[0]
    l = lax.dot_general(qh, kb, (((1,), (1,)), ((), ())), preferred_element_type=F32)
    neg_abs = lax.bitcast_convert_type(lax.bitcast_convert_type(l, jnp.uint32) | jnp.uint32(0x80000000), F32)
    lg = jnp.log2(1.0 + jnp.exp2(neg_abs))
    log_beta = jnp.minimum(l, 0.0) - lg
    log_fail = log_beta - l
    if strict is not None:
        log_fail = jnp.where(strict, log_fail, 0.0)
    inner = jnp.dot(log_fail.astype(BF16), suffix_mat, preferred_element_type=F32)
    w = jnp.exp2(log_beta + inner + jnp.concatenate([carry] * (tk // LANES), axis=1))
    if strict is not None:
        w = jnp.where(strict, w, 0.0)
    return w.astype(BF16), carry + jnp.sum(log_fail, axis=-1, keepdims=True)


def _sb_kernel(q_ref, k_ref, v_ref, o_ref, carry_ref, acc_ref):
    tq = SB_BLOCK
    i = pl.program_id(2)
    npairs = q_ref.shape[1] // LANES
    suffix_mat = jnp.where(lax.broadcasted_iota(jnp.int32, (tq, tq), 0) > lax.broadcasted_iota(jnp.int32, (tq, tq), 1),
                           1.0, 0.0).astype(BF16)
    lo_lanes = lax.broadcasted_iota(jnp.int32, (tq, LANES), 1) < SB_HEAD_DIM

    def pair_step(p, j, strict, rows=slice(0, SB_BLOCK)):
        start = pl.multiple_of(j * tq, tq)
        lanes = slice(p * LANES, (p + 1) * LANES)
        q = q_ref[rows, lanes]
        kb = k_ref[pl.ds(start, tq), lanes]
        vb = v_ref[pl.ds(start, tq), lanes]
        lo_q = lax.broadcasted_iota(jnp.int32, q.shape, 1) < SB_HEAD_DIM
        v_split = jnp.concatenate([jnp.where(lo_lanes, vb, jnp.zeros_like(vb)),
                                   jnp.where(lo_lanes, jnp.zeros_like(vb), vb)], axis=0)
        ws, carries = [], []
        for hh, qh in enumerate((jnp.where(lo_q, q, jnp.zeros_like(q)), jnp.where(lo_q, jnp.zeros_like(q), q))):
            w, carry = _sb_head_block(qh, kb, carry_ref[2 * p + hh, rows], suffix_mat, strict)
            carry_ref[2 * p + hh, rows] = carry
            ws.append(w)
            carries.append(carry)
        acc_ref[p, rows] += jnp.dot(jnp.concatenate(ws, axis=1), v_split, preferred_element_type=F32)
        return jnp.max(jnp.maximum(*carries))

    def step(j, strict):
        return tuple(pair_step(p, j, strict) for p in range(npairs))

    def alive(top):
        return top > -SB_UNDERFLOW_LOG2

    carry_ref[...] = jnp.zeros_like(carry_ref)
    acc_ref[...] = jnp.zeros_like(acc_ref)
    qpos = lax.broadcasted_iota(jnp.int32, (tq, tq), 0)
    kpos = lax.broadcasted_iota(jnp.int32, (tq, tq), 1)
    strict = kpos < qpos

    def first_two():
        step(i, strict)
        head, tail = slice(0, SB_NEIGHBOUR_ROWS), slice(SB_NEIGHBOUR_ROWS, tq)
        tops = tuple(pair_step(p, i - 1, None, head) for p in range(npairs))
        tail_top = jnp.max(carry_ref[:, tail, :])
        tail_tops = lax.cond(alive(tail_top), lambda: tuple(pair_step(p, i - 1, None, tail) for p in range(npairs)),
                             lambda: (tail_top,) * npairs)
        return tuple(jnp.maximum(a, b) for a, b in zip(tops, tail_tops))

    tops = lax.cond(i > 0, first_two, lambda: step(i, strict))


    def live(state):
        j, tops = state
        return jnp.logical_and(j >= 0, alive(functools.reduce(jnp.maximum, tops)))

    def body(state):
        j, tops = state
        return j - 1, tuple(lax.cond(alive(top), functools.partial(pair_step, p, j, None), lambda top=top: top)
                            for p, top in enumerate(tops))

    lax.while_loop(live, body, (i - 2, tops))
    for p in range(npairs):
        o_ref[:, p * LANES:(p + 1) * LANES] = acc_ref[p].astype(o_ref.dtype)


def _sb_attention(qkv, batch, seq):
    tq = SB_BLOCK
    nq = seq // tq
    width = SB_PAIRS_PER_STEP * LANES
    nsteps = D_MODEL // width
    return pl.pallas_call(
        _sb_kernel,
        grid=(batch, nsteps, nq),
        in_specs=[pl.BlockSpec((tq, width), lambda b, p, i: (b * nq + i, p)),
                  pl.BlockSpec((seq, width), lambda b, p, i: (b, nsteps + p)),
                  pl.BlockSpec((seq, width), lambda b, p, i: (b, 2 * nsteps + p))],
        out_specs=pl.BlockSpec((tq, width), lambda b, p, i: (b * nq + i, p)),
        out_shape=jax.ShapeDtypeStruct((batch * seq, D_MODEL), BF16),
        scratch_shapes=[pltpu.VMEM((2 * SB_PAIRS_PER_STEP, tq, LANES), F32),
                        pltpu.VMEM((SB_PAIRS_PER_STEP, tq, LANES), F32)],
        compiler_params=_params("arbitrary", "arbitrary", "arbitrary"),
        name="sb_attention",
    )(qkv, qkv, qkv)


def kernel(x, mix_norm, ffn_norm, final_norm, ssd_w_in, ssd_conv_w, ssd_conv_b, ssd_dt_bias, ssd_a_log, ssd_d, ssd_norm, ssd_w_out, sb_w_qkv, sb_w_out, ffn_w_in, ffn_conv_w, ffn_conv_b, ffn_w_out):
    batch, seq, d = x.shape
    xf = x.reshape(batch * seq, d)
    h = _rmsnorm(xf, mix_norm[0])
    out = None
    for i in range(DEPTH):
        j = i // 2
        if i % 2 == 0:
            w_dt = jnp.pad(ssd_w_in[j, :, SSD_D_INNER + SSD_CONV_DIM:], ((0, 0), (0, LANES - SSD_HEADS)))[None]
            z = _mm(h, ssd_w_in, j, 0, SSD_D_INNER, BF16, 2048, 1024, "ssd_z")
            xbc = _mm_conv_act(h, ssd_w_in, j, SSD_D_INNER, ssd_conv_w[j], ssd_conv_b[j], 1, seq, 1024, 1024, "ssd_xbc")
            dt_raw = _mm(h, w_dt, 0, 0, LANES, F32, 2048, LANES, "ssd_dt")
            mixed = _ssd_scan(z, xbc, dt_raw, ssd_dt_bias[j], ssd_a_log[j], ssd_d[j], ssd_norm[j], batch, seq)
            w_out = ssd_w_out
        else:
            qkv = _mm(h, sb_w_qkv, j, 0, 3 * d, BF16, 2048, 1024, "sb_qkv",
                      scaled_tiles=d // 1024, scale=LOG2E * SB_HEAD_DIM ** -0.5)
            mixed = _sb_attention(qkv, batch, seq)
            w_out = sb_w_out
        xf, h = _mm_res_norm(mixed, w_out, j, xf, ffn_norm[i], False, 1024, "mixer_out")
        act = _mm_conv_act(h, ffn_w_in, i, 0, ffn_conv_w[i], ffn_conv_b[i], 2, seq, 512, FFN_D_FF // 2, "ffn_in")
        if i + 1 < DEPTH:
            xf, h = _mm_res_norm(act, ffn_w_out, i, xf, mix_norm[i + 1], False, 1024, "ffn_out")
        else:
            out = _mm_res_norm(act, ffn_w_out, i, xf, final_norm, True, 1024, "ffn_out_final")
    return out.reshape(batch, seq, d)
```

```python
import functools
import math

import jax
import jax.numpy as jnp
from jax import lax
from jax.experimental import pallas as pl
from jax.experimental.pallas import tpu as pltpu

F32 = jnp.float32
BF16 = jnp.bfloat16

D_MODEL = 1024
DEPTH = 4
NORM_EPS = 1e-6
LOG2E = math.log2(math.e)

SSD_D_INNER = 2048
SSD_HEAD_DIM = 64
SSD_HEADS = 32
SSD_GROUPS = 8
SSD_HEADS_PER_GROUP = 4
SSD_STATE = 128
SSD_CHUNK = 128
SSD_CHUNKS_PER_STEP = 4
SSD_CONV_DIM = SSD_D_INNER + 2 * SSD_GROUPS * SSD_STATE
SSD_GROUP_WIDTH = SSD_HEADS_PER_GROUP * SSD_HEAD_DIM

SB_HEADS = 16
SB_HEAD_DIM = 64
SB_BLOCK = 256
SB_PAIRS_PER_STEP = 4
SB_UNDERFLOW_LOG2 = 160.0
SB_NEIGHBOUR_ROWS = 192

FFN_D_FF = 2816

MM_TILE = (2048, 1024)
XBC_TILE = (1024, 1024)
FFN_IN_TILE = (512, FFN_D_FF // 2)
OUT_ROW_TILE = 1024

LANES = 128
SUBLANES = 8
VMEM_LIMIT_BYTES = 56 * 1024 * 1024


def _params(*semantics):
    return pltpu.CompilerParams(dimension_semantics=semantics, vmem_limit_bytes=VMEM_LIMIT_BYTES)


def _rms_scale(v):
    return lax.rsqrt(jnp.mean(v * v, axis=-1, keepdims=True) + NORM_EPS)


def _rmsnorm_kernel(x_ref, g_ref, o_ref):
    x = x_ref[...]
    o_ref[...] = (x * _rms_scale(x) * g_ref[...]).astype(o_ref.dtype)


def _rmsnorm(x, g, tm=1024):
    t, d = x.shape
    return pl.pallas_call(
        _rmsnorm_kernel,
        grid=(t // tm,),
        in_specs=[pl.BlockSpec((tm, d), lambda i: (i, 0)), pl.BlockSpec((1, d), lambda i: (0, 0))],
        out_specs=pl.BlockSpec((tm, d), lambda i: (i, 0)),
        out_shape=jax.ShapeDtypeStruct((t, d), BF16),
        compiler_params=_params("arbitrary"),
        name="rmsnorm",
    )(x, g.reshape(1, d))


def _mm_kernel(h_ref, w_ref, o_ref, wb_ref, *, scaled_tiles, scale):
    @pl.when(pl.program_id(1) == 0)
    def _():
        w = w_ref[...]
        if scaled_tiles:
            w = w * jnp.where(pl.program_id(0) < scaled_tiles, scale, 1.0)
        wb_ref[...] = w.astype(BF16)

    o_ref[...] = jnp.dot(h_ref[...], wb_ref[...], preferred_element_type=F32).astype(o_ref.dtype)


def _mm(h, w, layer, col0, n, out_dtype, tm, tn, name, scaled_tiles=0, scale=1.0):
    t, k = h.shape
    off = col0 // tn
    return pl.pallas_call(
        functools.partial(_mm_kernel, scaled_tiles=scaled_tiles, scale=scale),
        grid=(n // tn, t // tm),
        in_specs=[pl.BlockSpec((tm, k), lambda j, i: (i, 0)),
                  pl.BlockSpec((None, k, tn), lambda j, i: (layer, 0, j + off), pipeline_mode=pl.Buffered(1))],
        out_specs=pl.BlockSpec((tm, tn), lambda j, i: (i, j)),
        out_shape=jax.ShapeDtypeStruct((t, n), out_dtype),
        scratch_shapes=[pltpu.VMEM((k, tn), BF16)],
        compiler_params=_params("arbitrary", "arbitrary"),
        name=name,
    )(h, w)


def _causal_conv(u3, prev, cw, cb):
    width = cw.shape[0]
    n = u3.shape[2]
    before = jnp.concatenate([prev, u3[:-1]], axis=0)
    sub = lax.broadcasted_iota(jnp.int32, (1, SUBLANES, n), 1)
    y = cb.reshape(1, 1, n) + cw[width - 1:width, :].reshape(1, 1, n) * u3
    for j in range(1, width):
        mixed = jnp.where(sub >= SUBLANES - j, before, u3)
        y = y + cw[width - 1 - j:width - j, :].reshape(1, 1, n) * pltpu.roll(mixed, j, axis=1)
    return y


def _silu(y):
    return y * (1.0 / (1.0 + jnp.exp2(y * -LOG2E)))


def _mm_conv_act_kernel(*refs, nbranch, tiles_per_seq):
    h_ref = refs[0]
    w_refs = refs[1:1 + nbranch]
    cw_refs = refs[1 + nbranch:1 + 2 * nbranch]
    cb_refs = refs[1 + 2 * nbranch:1 + 3 * nbranch]
    o_ref = refs[1 + 3 * nbranch]
    prev_refs = refs[2 + 3 * nbranch:2 + 4 * nbranch]
    wb_ref = refs[2 + 4 * nbranch]
    tm = h_ref.shape[0]
    tn = o_ref.shape[1]

    @pl.when(pl.program_id(1) == 0)
    def _():
        for b, w_ref in enumerate(w_refs):
            wb_ref[:, b * tn:(b + 1) * tn] = w_ref[...].astype(BF16)

    @pl.when(pl.program_id(1) % tiles_per_seq == 0)
    def _():
        for prev_ref in prev_refs:
            prev_ref[...] = jnp.zeros_like(prev_ref)

    u_all = jnp.dot(h_ref[...], wb_ref[...], preferred_element_type=F32)
    ys = []
    for b in range(nbranch):
        u = u_all[:, b * tn:(b + 1) * tn]
        u3 = u.reshape(tm // SUBLANES, SUBLANES, u.shape[1])
        ys.append(_causal_conv(u3, prev_refs[b][...], cw_refs[b][...], cb_refs[b][...]))
        prev_refs[b][...] = u3[tm // SUBLANES - 1:]
    act = _silu(ys[0])
    if nbranch == 2:
        act = act * ys[1]
    o_ref[...] = act.reshape(o_ref.shape).astype(o_ref.dtype)


def _mm_conv_act(h, w, layer, col0, cw, cb, nbranch, seq, tm, tn, name):
    t, k = h.shape
    n_out = cw.shape[1] // nbranch
    width = cw.shape[0]
    ncol = n_out // tn
    off = col0 // tn
    in_specs = [pl.BlockSpec((tm, k), lambda j, i: (i, 0))]
    in_specs += [pl.BlockSpec((None, k, tn), lambda j, i, b=b: (layer, 0, j + b * ncol + off),
                              pipeline_mode=pl.Buffered(1)) for b in range(nbranch)]
    in_specs += [pl.BlockSpec((width, tn), lambda j, i, b=b: (0, j + b * ncol)) for b in range(nbranch)]
    in_specs += [pl.BlockSpec((1, tn), lambda j, i, b=b: (0, j + b * ncol)) for b in range(nbranch)]
    cb2 = cb.reshape(1, -1)
    return pl.pallas_call(
        functools.partial(_mm_conv_act_kernel, nbranch=nbranch, tiles_per_seq=seq // tm),
        grid=(ncol, t // tm),
        in_specs=in_specs,
        out_specs=pl.BlockSpec((tm, tn), lambda j, i: (i, j)),
        out_shape=jax.ShapeDtypeStruct((t, n_out), BF16),
        scratch_shapes=[pltpu.VMEM((1, SUBLANES, tn), F32)] * nbranch + [pltpu.VMEM((k, nbranch * tn), BF16)],
        compiler_params=_params("arbitrary", "arbitrary"),
        name=name,
    )(h, *([w] * nbranch), *([cw] * nbranch), *([cb2] * nbranch))


def _residual_update(a_ref, w_ref, x_ref, wb_ref):
    @pl.when(pl.program_id(0) == 0)
    def _():
        wb_ref[...] = w_ref[...].astype(BF16)

    return x_ref[...] + jnp.dot(a_ref[...], wb_ref[...], preferred_element_type=F32)


def _mm_res_norm_kernel(a_ref, w_ref, x_ref, g_ref, xo_ref, ho_ref, wb_ref):
    xn = _residual_update(a_ref, w_ref, x_ref, wb_ref)
    xo_ref[...] = xn
    ho_ref[...] = (xn * _rms_scale(xn) * g_ref[...]).astype(ho_ref.dtype)


def _mm_res_final_kernel(a_ref, w_ref, x_ref, g_ref, o_ref, wb_ref):
    xn = _residual_update(a_ref, w_ref, x_ref, wb_ref)
    o_ref[...] = xn * _rms_scale(xn) * g_ref[...]


def _mm_res_norm(a, w, layer, x, g, final, tm, name):
    t, k = a.shape
    d = w.shape[2]
    row = pl.BlockSpec((tm, d), lambda i: (i, 0))
    in_specs = [pl.BlockSpec((tm, k), lambda i: (i, 0)),
                pl.BlockSpec((None, k, d), lambda i: (layer, 0, 0), pipeline_mode=pl.Buffered(1)),
                row, pl.BlockSpec((1, d), lambda i: (0, 0))]
    scratch = [pltpu.VMEM((k, d), BF16)]
    if final:
        return pl.pallas_call(
            _mm_res_final_kernel, grid=(t // tm,), in_specs=in_specs, out_specs=row,
            out_shape=jax.ShapeDtypeStruct((t, d), F32), scratch_shapes=scratch,
            compiler_params=_params("arbitrary"), name=name,
        )(a, w, x, g.reshape(1, d))
    return pl.pallas_call(
        _mm_res_norm_kernel, grid=(t // tm,), in_specs=in_specs, out_specs=[row, row],
        out_shape=[jax.ShapeDtypeStruct((t, d), F32), jax.ShapeDtypeStruct((t, d), BF16)], scratch_shapes=scratch,
        compiler_params=_params("arbitrary"), name=name,
    )(a, w, x, g.reshape(1, d))


def _split3(v):
    a = v.astype(BF16)
    r = v - a.astype(F32)
    b = r.astype(BF16)
    c = (r - b.astype(F32)).astype(BF16)
    return a, b, c


def _expand_heads(v, h0, nheads):
    rows = v.shape[0]
    lane = lax.broadcasted_iota(jnp.int32, (rows, LANES), 1)
    pairs = []
    for p in range(nheads // 2):
        if rows % SUBLANES:
            lo = jnp.broadcast_to(v[:, h0 + 2 * p:h0 + 2 * p + 1], (rows, LANES))
            hi = jnp.broadcast_to(v[:, h0 + 2 * p + 1:h0 + 2 * p + 2], (rows, LANES))
            pairs.append(jnp.where(lane < SSD_HEAD_DIM, lo, hi))
        else:
            pairs.append(jnp.take_along_axis(v, h0 + 2 * p + lane // SSD_HEAD_DIM, axis=1))
    return jnp.concatenate(pairs, axis=1)


def _ssd_kernel(z_ref, xbc_ref, dt_ref, dtb_ref, alog_ref, dskip_ref, ng_ref, o_ref, state_ref, y_ref):
    @pl.when(pl.program_id(1) == 0)
    def _():
        state_ref[...] = jnp.zeros_like(state_ref)

    for c in range(SSD_CHUNKS_PER_STEP):
        _ssd_chunk(slice(c * SSD_CHUNK, (c + 1) * SSD_CHUNK), z_ref, xbc_ref, dt_ref, dtb_ref, alog_ref, dskip_ref,
                   ng_ref, o_ref, state_ref, y_ref)


def _ssd_chunk(rows, z_ref, xbc_ref, dt_ref, dtb_ref, alog_ref, dskip_ref, ng_ref, o_ref, state_ref, y_ref):
    L = SSD_CHUNK
    pre = dt_ref[rows, :] + dtb_ref[...]
    dt = jnp.maximum(pre, 0.0) + jnp.log1p(jnp.exp(-jnp.abs(pre)))
    a = (-LOG2E * jnp.exp(alog_ref[...])) * dt
    row = lax.broadcasted_iota(jnp.int32, (L, L), 0)
    col = lax.broadcasted_iota(jnp.int32, (L, L), 1)
    causal = row >= col
    tri = jnp.where(causal, 1.0, 0.0).astype(BF16)
    a1, a2, a3 = _split3(a)
    acum = (jnp.dot(tri, a1, preferred_element_type=F32) + jnp.dot(tri, a2, preferred_element_type=F32)
            + jnp.dot(tri, a3, preferred_element_type=F32))
    src_t = (acum - jnp.log2(jnp.maximum(dt, jnp.finfo(F32).tiny))).T
    last = acum[L - 1:L, :]
    exp_acum = jnp.exp2(acum)
    w_dt = jnp.exp2(last - acum) * dt
    exp_last = jnp.exp2(last)
    lane = lax.broadcasted_iota(jnp.int32, (L, LANES), 1)
    lo_lanes = lane < SSD_HEAD_DIM

    for g in range(SSD_GROUPS):
        b_g = xbc_ref[rows, SSD_D_INNER + g * SSD_STATE:SSD_D_INNER + (g + 1) * SSD_STATE]
        c_off = SSD_D_INNER + SSD_GROUPS * SSD_STATE
        c_g = xbc_ref[rows, c_off + g * SSD_STATE:c_off + (g + 1) * SSD_STATE]
        cb = lax.dot_general(c_g, b_g, (((1,), (1,)), ((), ())), preferred_element_type=F32)
        x0 = g * SSD_GROUP_WIDTH
        h0 = g * SSD_HEADS_PER_GROUP
        y_diag = []
        for p in range(SSD_HEADS_PER_GROUP // 2):
            scores = []
            for hh in (h0 + 2 * p, h0 + 2 * p + 1):
                seg = acum[:, hh:hh + 1] - src_t[hh:hh + 1, :]
                scores.append((cb * jnp.exp2(jnp.where(causal, seg, -jnp.inf))).astype(BF16))
            xs = xbc_ref[rows, x0 + p * LANES:x0 + (p + 1) * LANES]
            zero = jnp.zeros_like(xs)
            xs_split = jnp.concatenate([jnp.where(lo_lanes, xs, zero), jnp.where(lo_lanes, zero, xs)], axis=0)
            y_diag.append(jnp.dot(jnp.concatenate(scores, axis=1), xs_split, preferred_element_type=F32))
        y_diag = jnp.concatenate(y_diag, axis=1)

        xs_g = xbc_ref[rows, x0:x0 + SSD_GROUP_WIDTH].astype(F32)
        state = state_ref[g]
        y_off = (jnp.dot(c_g, state.astype(BF16), preferred_element_type=F32)
                 * _expand_heads(exp_acum, h0, SSD_HEADS_PER_GROUP))
        xw = (xs_g * _expand_heads(w_dt, h0, SSD_HEADS_PER_GROUP)).astype(BF16)
        state_ref[g] = (state * _expand_heads(exp_last, h0, SSD_HEADS_PER_GROUP)
                        + lax.dot_general(b_g, xw, (((0,), (0,)), ((), ())), preferred_element_type=F32))
        y_ref[rows, x0:x0 + SSD_GROUP_WIDTH] = y_diag + y_off + dskip_ref[:, x0:x0 + SSD_GROUP_WIDTH] * xs_g

    z = z_ref[rows, :].astype(F32)
    gated = y_ref[rows, :] * _silu(z)
    o_ref[rows, :] = (gated * _rms_scale(gated) * ng_ref[...]).astype(o_ref.dtype)


def _ssd_scan(z, xbc, dt_raw, dt_bias, a_log, d_skip, norm_g, batch, seq):
    L = SSD_CHUNKS_PER_STEP * SSD_CHUNK
    nc = seq // L
    pad = LANES - SSD_HEADS
    dtb = jnp.pad(dt_bias, (0, pad)).reshape(1, LANES)
    alog = jnp.pad(a_log, (0, pad)).reshape(1, LANES)
    dskip = jnp.repeat(d_skip, SSD_HEAD_DIM).reshape(1, SSD_D_INNER)
    small = lambda n: pl.BlockSpec((1, n), lambda b, c: (0, 0))
    return pl.pallas_call(
        _ssd_kernel,
        grid=(batch, nc),
        in_specs=[pl.BlockSpec((L, SSD_D_INNER), lambda b, c: (b * nc + c, 0)),
                  pl.BlockSpec((L, SSD_CONV_DIM), lambda b, c: (b * nc + c, 0)),
                  pl.BlockSpec((L, LANES), lambda b, c: (b * nc + c, 0)),
                  small(LANES), small(LANES), small(SSD_D_INNER), small(SSD_D_INNER)],
        out_specs=pl.BlockSpec((L, SSD_D_INNER), lambda b, c: (b * nc + c, 0)),
        out_shape=jax.ShapeDtypeStruct((batch * seq, SSD_D_INNER), BF16),
        scratch_shapes=[pltpu.VMEM((SSD_GROUPS, SSD_STATE, SSD_GROUP_WIDTH), F32),
                        pltpu.VMEM((L, SSD_D_INNER), F32)],
        compiler_params=_params("arbitrary", "arbitrary"),
        name="ssd_scan",
    )(z, xbc, dt_raw, dtb, alog, dskip, norm_g.reshape(1, SSD_D_INNER))


def _sb_head_block(qh, kb, carry, suffix_mat, strict):
    tk = kb.shape[0]
    l = lax.dot_general(qh, kb, (((1,), (1,)), ((), ())), preferred_element_type=F32)
    neg_abs = lax.bitcast_convert_type(lax.bitcast_convert_type(l, jnp.uint32) | jnp.uint32(0x80000000), F32)
    lg = jnp.log2(1.0 + jnp.exp2(neg_abs))
    log_beta = jnp.minimum(l, 0.0) - lg
    log_fail = log_beta - l
    if strict is not None:
        log_fail = jnp.where(strict, log_fail, 0.0)
    inner = jnp.dot(log_fail.astype(BF16), suffix_mat, preferred_element_type=F32)
    w = jnp.exp2(log_beta + inner + jnp.concatenate([carry] * (tk // LANES), axis=1))
    if strict is not None:
        w = jnp.where(strict, w, 0.0)
    return w.astype(BF16), carry + jnp.sum(log_fail, axis=-1, keepdims=True)


def _sb_kernel(q_ref, k_ref, v_ref, o_ref, carry_ref, acc_ref):
    tq = SB_BLOCK
    i = pl.program_id(2)
    npairs = q_ref.shape[1] // LANES
    suffix_mat = jnp.where(lax.broadcasted_iota(jnp.int32, (tq, tq), 0) > lax.broadcasted_iota(jnp.int32, (tq, tq), 1),
                           1.0, 0.0).astype(BF16)
    lo_lanes = lax.broadcasted_iota(jnp.int32, (tq, LANES), 1) < SB_HEAD_DIM

    def pair_step(p, j, strict, rows=slice(0, SB_BLOCK)):
        start = pl.multiple_of(j * tq, tq)
        lanes = slice(p * LANES, (p + 1) * LANES)
        q = q_ref[rows, lanes]
        kb = k_ref[pl.ds(start, tq), lanes]
        vb = v_ref[pl.ds(start, tq), lanes]
        lo_q = lax.broadcasted_iota(jnp.int32, q.shape, 1) < SB_HEAD_DIM
        v_split = jnp.concatenate([jnp.where(lo_lanes, vb, jnp.zeros_like(vb)),
                                   jnp.where(lo_lanes, jnp.zeros_like(vb), vb)], axis=0)
        ws, carries = [], []
        for hh, qh in enumerate((jnp.where(lo_q, q, jnp.zeros_like(q)), jnp.where(lo_q, jnp.zeros_like(q), q))):
            w, carry = _sb_head_block(qh, kb, carry_ref[2 * p + hh, rows], suffix_mat, strict)
            carry_ref[2 * p + hh, rows] = carry
            ws.append(w)
            carries.append(carry)
        acc_ref[p, rows] += jnp.dot(jnp.concatenate(ws, axis=1), v_split, preferred_element_type=F32)
        return jnp.max(jnp.maximum(*carries))

    def step(j, strict, rows=slice(0, SB_BLOCK)):
        return functools.reduce(jnp.maximum, [pair_step(p, j, strict, rows) for p in range(npairs)])

    def alive(top):
        return top > -SB_UNDERFLOW_LOG2

    carry_ref[...] = jnp.zeros_like(carry_ref)
    acc_ref[...] = jnp.zeros_like(acc_ref)
    qpos = lax.broadcasted_iota(jnp.int32, (tq, tq), 0)
    kpos = lax.broadcasted_iota(jnp.int32, (tq, tq), 1)
    strict = kpos < qpos

    def first_two():
        step(i, strict)
        top = step(i - 1, None, slice(0, SB_NEIGHBOUR_ROWS))
        tail = slice(SB_NEIGHBOUR_ROWS, tq)
        tail_top = jnp.max(carry_ref[:, tail, :])
        tail_top = lax.cond(alive(tail_top), lambda: step(i - 1, None, tail), lambda: tail_top)
        return jnp.maximum(top, tail_top)

    top = lax.cond(i > 0, first_two, lambda: step(i, strict))

    def live(state):
        j, top = state
        return jnp.logical_and(j >= 0, alive(top))

    def body(state):
        j, _ = state
        return j - 1, step(j, None)

    lax.while_loop(live, body, (i - 2, top))
    for p in range(npairs):
        o_ref[:, p * LANES:(p + 1) * LANES] = acc_ref[p].astype(o_ref.dtype)


def _sb_attention(qkv, batch, seq):
    tq = SB_BLOCK
    nq = seq // tq
    width = SB_PAIRS_PER_STEP * LANES
    nsteps = D_MODEL // width
    return pl.pallas_call(
        _sb_kernel,
        grid=(batch, nsteps, nq),
        in_specs=[pl.BlockSpec((tq, width), lambda b, p, i: (b * nq + i, p)),
                  pl.BlockSpec((seq, width), lambda b, p, i: (b, nsteps + p)),
                  pl.BlockSpec((seq, width), lambda b, p, i: (b, 2 * nsteps + p))],
        out_specs=pl.BlockSpec((tq, width), lambda b, p, i: (b * nq + i, p)),
        out_shape=jax.ShapeDtypeStruct((batch * seq, D_MODEL), BF16),
        scratch_shapes=[pltpu.VMEM((2 * SB_PAIRS_PER_STEP, tq, LANES), F32),
                        pltpu.VMEM((SB_PAIRS_PER_STEP, tq, LANES), F32)],
        compiler_params=_params("arbitrary", "arbitrary", "arbitrary"),
        name="sb_attention",
    )(qkv, qkv, qkv)


def kernel(x, mix_norm, ffn_norm, final_norm, ssd_w_in, ssd_conv_w, ssd_conv_b, ssd_dt_bias, ssd_a_log, ssd_d, ssd_norm, ssd_w_out, sb_w_qkv, sb_w_out, ffn_w_in, ffn_conv_w, ffn_conv_b, ffn_w_out):
    batch, seq, d = x.shape
    xf = x.reshape(batch * seq, d)
    h = _rmsnorm(xf, mix_norm[0])
    out = None
    for i in range(DEPTH):
        j = i // 2
        if i % 2 == 0:
            w_dt = jnp.pad(ssd_w_in[j, :, SSD_D_INNER + SSD_CONV_DIM:], ((0, 0), (0, LANES - SSD_HEADS)))[None]
            z = _mm(h, ssd_w_in, j, 0, SSD_D_INNER, BF16, *MM_TILE, "ssd_z")
            xbc = _mm_conv_act(h, ssd_w_in, j, SSD_D_INNER, ssd_conv_w[j], ssd_conv_b[j], 1, seq, *XBC_TILE, "ssd_xbc")
            dt_raw = _mm(h, w_dt, 0, 0, LANES, F32, MM_TILE[0], LANES, "ssd_dt")
            mixed = _ssd_scan(z, xbc, dt_raw, ssd_dt_bias[j], ssd_a_log[j], ssd_d[j], ssd_norm[j], batch, seq)
            w_out = ssd_w_out
        else:
            qkv = _mm(h, sb_w_qkv, j, 0, 3 * d, BF16, *MM_TILE, "sb_qkv",
                      scaled_tiles=d // MM_TILE[1], scale=LOG2E * SB_HEAD_DIM ** -0.5)
            mixed = _sb_attention(qkv, batch, seq)
            w_out = sb_w_out
        xf, h = _mm_res_norm(mixed, w_out, j, xf, ffn_norm[i], False, OUT_ROW_TILE, "mixer_out")
        act = _mm_conv_act(h, ffn_w_in, i, 0, ffn_conv_w[i], ffn_conv_b[i], 2, seq, *FFN_IN_TILE, "ffn_in")
        if i + 1 < DEPTH:
            xf, h = _mm_res_norm(act, ffn_w_out, i, xf, mix_norm[i + 1], False, OUT_ROW_TILE, "ffn_out")
        else:
            out = _mm_res_norm(act, ffn_w_out, i, xf, final_norm, True, OUT_ROW_TILE, "ffn_out_final")
    return out.reshape(batch, seq, d)
```

```python
import functools
import math

import jax
import jax.numpy as jnp
from jax import lax
from jax.experimental import pallas as pl
from jax.experimental.pallas import tpu as pltpu

F32 = jnp.float32
BF16 = jnp.bfloat16

D_MODEL = 1024
DEPTH = 4
NORM_EPS = 1e-6
LOG2E = math.log2(math.e)

SSD_D_INNER = 2048
SSD_HEAD_DIM = 64
SSD_HEADS = 32
SSD_GROUPS = 8
SSD_HEADS_PER_GROUP = 4
SSD_STATE = 128
SSD_CHUNK = 128
SSD_CHUNKS_PER_STEP = 4
SSD_CONV_DIM = SSD_D_INNER + 2 * SSD_GROUPS * SSD_STATE
SSD_GROUP_WIDTH = SSD_HEADS_PER_GROUP * SSD_HEAD_DIM

SB_HEADS = 16
SB_HEAD_DIM = 64
SB_BLOCK = 256
SB_PAIRS_PER_STEP = 8
SB_UNDERFLOW_LOG2 = 160.0
SB_NEIGHBOUR_ROWS = 192

FFN_D_FF = 2816

MM_TILE = (2048, 1024)
XBC_TILE = (1024, 1024)
FFN_IN_TILE = (512, FFN_D_FF // 2)
OUT_ROW_TILE = 1024

LANES = 128
SUBLANES = 8
VMEM_LIMIT_BYTES = 56 * 1024 * 1024


def _params(*semantics):
    return pltpu.CompilerParams(dimension_semantics=semantics, vmem_limit_bytes=VMEM_LIMIT_BYTES)


def _rms_scale(v):
    return lax.rsqrt(jnp.mean(v * v, axis=-1, keepdims=True) + NORM_EPS)


def _rmsnorm_kernel(x_ref, g_ref, o_ref):
    x = x_ref[...]
    o_ref[...] = (x * _rms_scale(x) * g_ref[...]).astype(o_ref.dtype)


def _rmsnorm(x, g, tm=1024):
    t, d = x.shape
    return pl.pallas_call(
        _rmsnorm_kernel,
        grid=(t // tm,),
        in_specs=[pl.BlockSpec((tm, d), lambda i: (i, 0)), pl.BlockSpec((1, d), lambda i: (0, 0))],
        out_specs=pl.BlockSpec((tm, d), lambda i: (i, 0)),
        out_shape=jax.ShapeDtypeStruct((t, d), BF16),
        compiler_params=_params("arbitrary"),
        name="rmsnorm",
    )(x, g.reshape(1, d))


def _mm_kernel(h_ref, w_ref, o_ref, wb_ref, *, scaled_tiles, scale):
    @pl.when(pl.program_id(1) == 0)
    def _():
        w = w_ref[...]
        if scaled_tiles:
            w = w * jnp.where(pl.program_id(0) < scaled_tiles, scale, 1.0)
        wb_ref[...] = w.astype(BF16)

    o_ref[...] = jnp.dot(h_ref[...], wb_ref[...], preferred_element_type=F32).astype(o_ref.dtype)


def _mm(h, w, layer, col0, n, out_dtype, tm, tn, name, scaled_tiles=0, scale=1.0):
    t, k = h.shape
    off = col0 // tn
    return pl.pallas_call(
        functools.partial(_mm_kernel, scaled_tiles=scaled_tiles, scale=scale),
        grid=(n // tn, t // tm),
        in_specs=[pl.BlockSpec((tm, k), lambda j, i: (i, 0)),
                  pl.BlockSpec((None, k, tn), lambda j, i: (layer, 0, j + off), pipeline_mode=pl.Buffered(1))],
        out_specs=pl.BlockSpec((tm, tn), lambda j, i: (i, j)),
        out_shape=jax.ShapeDtypeStruct((t, n), out_dtype),
        scratch_shapes=[pltpu.VMEM((k, tn), BF16)],
        compiler_params=_params("arbitrary", "arbitrary"),
        name=name,
    )(h, w)


def _causal_conv(u3, prev, cw, cb):
    width = cw.shape[0]
    n = u3.shape[2]
    before = jnp.concatenate([prev, u3[:-1]], axis=0)
    sub = lax.broadcasted_iota(jnp.int32, (1, SUBLANES, n), 1)
    y = cb.reshape(1, 1, n) + cw[width - 1:width, :].reshape(1, 1, n) * u3
    for j in range(1, width):
        mixed = jnp.where(sub >= SUBLANES - j, before, u3)
        y = y + cw[width - 1 - j:width - j, :].reshape(1, 1, n) * pltpu.roll(mixed, j, axis=1)
    return y


def _silu(y):
    return y * (1.0 / (1.0 + jnp.exp2(y * -LOG2E)))


def _mm_conv_act_kernel(*refs, nbranch, tiles_per_seq):
    h_ref = refs[0]
    w_refs = refs[1:1 + nbranch]
    cw_refs = refs[1 + nbranch:1 + 2 * nbranch]
    cb_refs = refs[1 + 2 * nbranch:1 + 3 * nbranch]
    o_ref = refs[1 + 3 * nbranch]
    prev_refs = refs[2 + 3 * nbranch:2 + 4 * nbranch]
    wb_ref = refs[2 + 4 * nbranch]
    tm = h_ref.shape[0]
    tn = o_ref.shape[1]

    @pl.when(pl.program_id(1) == 0)
    def _():
        for b, w_ref in enumerate(w_refs):
            wb_ref[:, b * tn:(b + 1) * tn] = w_ref[...].astype(BF16)

    @pl.when(pl.program_id(1) % tiles_per_seq == 0)
    def _():
        for prev_ref in prev_refs:
            prev_ref[...] = jnp.zeros_like(prev_ref)

    u_all = jnp.dot(h_ref[...], wb_ref[...], preferred_element_type=F32)
    ys = []
    for b in range(nbranch):
        u = u_all[:, b * tn:(b + 1) * tn]
        u3 = u.reshape(tm // SUBLANES, SUBLANES, u.shape[1])
        ys.append(_causal_conv(u3, prev_refs[b][...], cw_refs[b][...], cb_refs[b][...]))
        prev_refs[b][...] = u3[tm // SUBLANES - 1:]
    act = _silu(ys[0])
    if nbranch == 2:
        act = act * ys[1]
    o_ref[...] = act.reshape(o_ref.shape).astype(o_ref.dtype)


def _mm_conv_act(h, w, layer, col0, cw, cb, nbranch, seq, tm, tn, name):
    t, k = h.shape
    n_out = cw.shape[1] // nbranch
    width = cw.shape[0]
    ncol = n_out // tn
    off = col0 // tn
    in_specs = [pl.BlockSpec((tm, k), lambda j, i: (i, 0))]
    in_specs += [pl.BlockSpec((None, k, tn), lambda j, i, b=b: (layer, 0, j + b * ncol + off),
                              pipeline_mode=pl.Buffered(1)) for b in range(nbranch)]
    in_specs += [pl.BlockSpec((width, tn), lambda j, i, b=b: (0, j + b * ncol)) for b in range(nbranch)]
    in_specs += [pl.BlockSpec((1, tn), lambda j, i, b=b: (0, j + b * ncol)) for b in range(nbranch)]
    cb2 = cb.reshape(1, -1)
    return pl.pallas_call(
        functools.partial(_mm_conv_act_kernel, nbranch=nbranch, tiles_per_seq=seq // tm),
        grid=(ncol, t // tm),
        in_specs=in_specs,
        out_specs=pl.BlockSpec((tm, tn), lambda j, i: (i, j)),
        out_shape=jax.ShapeDtypeStruct((t, n_out), BF16),
        scratch_shapes=[pltpu.VMEM((1, SUBLANES, tn), F32)] * nbranch + [pltpu.VMEM((k, nbranch * tn), BF16)],
        compiler_params=_params("arbitrary", "arbitrary"),
        name=name,
    )(h, *([w] * nbranch), *([cw] * nbranch), *([cb2] * nbranch))


def _residual_update(a_ref, w_ref, x_ref, wb_ref):
    @pl.when(pl.program_id(0) == 0)
    def _():
        wb_ref[...] = w_ref[...].astype(BF16)

    return x_ref[...] + jnp.dot(a_ref[...], wb_ref[...], preferred_element_type=F32)


def _mm_res_norm_kernel(a_ref, w_ref, x_ref, g_ref, xo_ref, ho_ref, wb_ref):
    xn = _residual_update(a_ref, w_ref, x_ref, wb_ref)
    xo_ref[...] = xn
    ho_ref[...] = (xn * _rms_scale(xn) * g_ref[...]).astype(ho_ref.dtype)


def _mm_res_final_kernel(a_ref, w_ref, x_ref, g_ref, o_ref, wb_ref):
    xn = _residual_update(a_ref, w_ref, x_ref, wb_ref)
    o_ref[...] = xn * _rms_scale(xn) * g_ref[...]


def _mm_res_norm(a, w, layer, x, g, final, tm, name):
    t, k = a.shape
    d = w.shape[2]
    row = pl.BlockSpec((tm, d), lambda i: (i, 0))
    in_specs = [pl.BlockSpec((tm, k), lambda i: (i, 0)),
                pl.BlockSpec((None, k, d), lambda i: (layer, 0, 0), pipeline_mode=pl.Buffered(1)),
                row, pl.BlockSpec((1, d), lambda i: (0, 0))]
    scratch = [pltpu.VMEM((k, d), BF16)]
    if final:
        return pl.pallas_call(
            _mm_res_final_kernel, grid=(t // tm,), in_specs=in_specs, out_specs=row,
            out_shape=jax.ShapeDtypeStruct((t, d), F32), scratch_shapes=scratch,
            compiler_params=_params("arbitrary"), name=name,
        )(a, w, x, g.reshape(1, d))
    return pl.pallas_call(
        _mm_res_norm_kernel, grid=(t // tm,), in_specs=in_specs, out_specs=[row, row],
        out_shape=[jax.ShapeDtypeStruct((t, d), F32), jax.ShapeDtypeStruct((t, d), BF16)], scratch_shapes=scratch,
        compiler_params=_params("arbitrary"), name=name,
    )(a, w, x, g.reshape(1, d))


def _split3(v):
    a = v.astype(BF16)
    r = v - a.astype(F32)
    b = r.astype(BF16)
    c = (r - b.astype(F32)).astype(BF16)
    return a, b, c


def _expand_heads(v, h0, nheads):
    rows = v.shape[0]
    lane = lax.broadcasted_iota(jnp.int32, (rows, LANES), 1)
    pairs = []
    for p in range(nheads // 2):
        if rows % SUBLANES:
            lo = jnp.broadcast_to(v[:, h0 + 2 * p:h0 + 2 * p + 1], (rows, LANES))
            hi = jnp.broadcast_to(v[:, h0 + 2 * p + 1:h0 + 2 * p + 2], (rows, LANES))
            pairs.append(jnp.where(lane < SSD_HEAD_DIM, lo, hi))
        else:
            pairs.append(jnp.take_along_axis(v, h0 + 2 * p + lane // SSD_HEAD_DIM, axis=1))
    return jnp.concatenate(pairs, axis=1)


def _ssd_kernel(z_ref, xbc_ref, dt_ref, dtb_ref, alog_ref, dskip_ref, ng_ref, o_ref, state_ref, y_ref):
    @pl.when(pl.program_id(1) == 0)
    def _():
        state_ref[...] = jnp.zeros_like(state_ref)

    for c in range(SSD_CHUNKS_PER_STEP):
        _ssd_chunk(slice(c * SSD_CHUNK, (c + 1) * SSD_CHUNK), z_ref, xbc_ref, dt_ref, dtb_ref, alog_ref, dskip_ref,
                   ng_ref, o_ref, state_ref, y_ref)


def _ssd_chunk(rows, z_ref, xbc_ref, dt_ref, dtb_ref, alog_ref, dskip_ref, ng_ref, o_ref, state_ref, y_ref):
    L = SSD_CHUNK
    pre = dt_ref[rows, :] + dtb_ref[...]
    dt = jnp.maximum(pre, 0.0) + jnp.log1p(jnp.exp(-jnp.abs(pre)))
    a = (-LOG2E * jnp.exp(alog_ref[...])) * dt
    row = lax.broadcasted_iota(jnp.int32, (L, L), 0)
    col = lax.broadcasted_iota(jnp.int32, (L, L), 1)
    causal = row >= col
    tri = jnp.where(causal, 1.0, 0.0).astype(BF16)
    a1, a2, a3 = _split3(a)
    acum = (jnp.dot(tri, a1, preferred_element_type=F32) + jnp.dot(tri, a2, preferred_element_type=F32)
            + jnp.dot(tri, a3, preferred_element_type=F32))
    src_t = (acum - jnp.log2(jnp.maximum(dt, jnp.finfo(F32).tiny))).T
    last = acum[L - 1:L, :]
    exp_acum = jnp.exp2(acum)
    w_dt = jnp.exp2(last - acum) * dt
    exp_last = jnp.exp2(last)
    lane = lax.broadcasted_iota(jnp.int32, (L, LANES), 1)
    lo_lanes = lane < SSD_HEAD_DIM

    for g in range(SSD_GROUPS):
        b_g = xbc_ref[rows, SSD_D_INNER + g * SSD_STATE:SSD_D_INNER + (g + 1) * SSD_STATE]
        c_off = SSD_D_INNER + SSD_GROUPS * SSD_STATE
        c_g = xbc_ref[rows, c_off + g * SSD_STATE:c_off + (g + 1) * SSD_STATE]
        cb = lax.dot_general(c_g, b_g, (((1,), (1,)), ((), ())), preferred_element_type=F32)
        x0 = g * SSD_GROUP_WIDTH
        h0 = g * SSD_HEADS_PER_GROUP
        y_diag = []
        for p in range(SSD_HEADS_PER_GROUP // 2):
            scores = []
            for hh in (h0 + 2 * p, h0 + 2 * p + 1):
                seg = acum[:, hh:hh + 1] - src_t[hh:hh + 1, :]
                scores.append((cb * jnp.exp2(jnp.where(causal, seg, -jnp.inf))).astype(BF16))
            xs = xbc_ref[rows, x0 + p * LANES:x0 + (p + 1) * LANES]
            zero = jnp.zeros_like(xs)
            xs_split = jnp.concatenate([jnp.where(lo_lanes, xs, zero), jnp.where(lo_lanes, zero, xs)], axis=0)
            y_diag.append(jnp.dot(jnp.concatenate(scores, axis=1), xs_split, preferred_element_type=F32))
        y_diag = jnp.concatenate(y_diag, axis=1)

        xs_g = xbc_ref[rows, x0:x0 + SSD_GROUP_WIDTH].astype(F32)
        state = state_ref[g]
        y_off = (jnp.dot(c_g, state.astype(BF16), preferred_element_type=F32)
                 * _expand_heads(exp_acum, h0, SSD_HEADS_PER_GROUP))
        xw = (xs_g * _expand_heads(w_dt, h0, SSD_HEADS_PER_GROUP)).astype(BF16)
        state_ref[g] = (state * _expand_heads(exp_last, h0, SSD_HEADS_PER_GROUP)
                        + lax.dot_general(b_g, xw, (((0,), (0,)), ((), ())), preferred_element_type=F32))
        y_ref[rows, x0:x0 + SSD_GROUP_WIDTH] = y_diag + y_off + dskip_ref[:, x0:x0 + SSD_GROUP_WIDTH] * xs_g

    z = z_ref[rows, :].astype(F32)
    gated = y_ref[rows, :] * _silu(z)
    o_ref[rows, :] = (gated * _rms_scale(gated) * ng_ref[...]).astype(o_ref.dtype)


def _ssd_scan(z, xbc, dt_raw, dt_bias, a_log, d_skip, norm_g, batch, seq):
    L = SSD_CHUNKS_PER_STEP * SSD_CHUNK
    nc = seq // L
    pad = LANES - SSD_HEADS
    dtb = jnp.pad(dt_bias, (0, pad)).reshape(1, LANES)
    alog = jnp.pad(a_log, (0, pad)).reshape(1, LANES)
    dskip = jnp.repeat(d_skip, SSD_HEAD_DIM).reshape(1, SSD_D_INNER)
    small = lambda n: pl.BlockSpec((1, n), lambda b, c: (0, 0))
    return pl.pallas_call(
        _ssd_kernel,
        grid=(batch, nc),
        in_specs=[pl.BlockSpec((L, SSD_D_INNER), lambda b, c: (b * nc + c, 0)),
                  pl.BlockSpec((L, SSD_CONV_DIM), lambda b, c: (b * nc + c, 0)),
                  pl.BlockSpec((L, LANES), lambda b, c: (b * nc + c, 0)),
                  small(LANES), small(LANES), small(SSD_D_INNER), small(SSD_D_INNER)],
        out_specs=pl.BlockSpec((L, SSD_D_INNER), lambda b, c: (b * nc + c, 0)),
        out_shape=jax.ShapeDtypeStruct((batch * seq, SSD_D_INNER), BF16),
        scratch_shapes=[pltpu.VMEM((SSD_GROUPS, SSD_STATE, SSD_GROUP_WIDTH), F32),
                        pltpu.VMEM((L, SSD_D_INNER), F32)],
        compiler_params=_params("arbitrary", "arbitrary"),
        name="ssd_scan",
    )(z, xbc, dt_raw, dtb, alog, dskip, norm_g.reshape(1, SSD_D_INNER))


def _sb_head_block(qh, kb, carry, suffix_mat, strict):
    tk = kb.shape[0]
    l = lax.dot_general(qh, kb, (((1,), (1,)), ((), ())), preferred_element_type=F32)
    neg_abs = lax.bitcast_convert_type(lax.bitcast_convert_type(l, jnp.uint32) | jnp.uint32(0x80000000), F32)
    lg = jnp.log2(1.0 + jnp.exp2(neg_abs))
    log_beta = jnp.minimum(l, 0.0) - lg
    log_fail = log_beta - l
    if strict is not None:
        log_fail = jnp.where(strict, log_fail, 0.0)
    inner = jnp.dot(log_fail.astype(BF16), suffix_mat, preferred_element_type=F32)
    w = jnp.exp2(log_beta + inner + jnp.concatenate([carry] * (tk // LANES), axis=1))
    if strict is not None:
        w = jnp.where(strict, w, 0.0)
    return w.astype(BF16), carry + jnp.sum(log_fail, axis=-1, keepdims=True)


def _sb_kernel(q_ref, k_ref, v_ref, o_ref, carry_ref, acc_ref):
    tq = SB_BLOCK
    i = pl.program_id(2)
    npairs = q_ref.shape[1] // LANES
    suffix_mat = jnp.where(lax.broadcasted_iota(jnp.int32, (tq, tq), 0) > lax.broadcasted_iota(jnp.int32, (tq, tq), 1),
                           1.0, 0.0).astype(BF16)
    lo_lanes = lax.broadcasted_iota(jnp.int32, (tq, LANES), 1) < SB_HEAD_DIM

    def pair_step(p, j, strict, rows=slice(0, SB_BLOCK)):
        start = pl.multiple_of(j * tq, tq)
        lanes = slice(p * LANES, (p + 1) * LANES)
        q = q_ref[rows, lanes]
        kb = k_ref[pl.ds(start, tq), lanes]
        vb = v_ref[pl.ds(start, tq), lanes]
        lo_q = lax.broadcasted_iota(jnp.int32, q.shape, 1) < SB_HEAD_DIM
        v_split = jnp.concatenate([jnp.where(lo_lanes, vb, jnp.zeros_like(vb)),
                                   jnp.where(lo_lanes, jnp.zeros_like(vb), vb)], axis=0)
        ws, carries = [], []
        for hh, qh in enumerate((jnp.where(lo_q, q, jnp.zeros_like(q)), jnp.where(lo_q, jnp.zeros_like(q), q))):
            w, carry = _sb_head_block(qh, kb, carry_ref[2 * p + hh, rows], suffix_mat, strict)
            carry_ref[2 * p + hh, rows] = carry
            ws.append(w)
            carries.append(carry)
        acc_ref[p, rows] += jnp.dot(jnp.concatenate(ws, axis=1), v_split, preferred_element_type=F32)
        return jnp.max(jnp.maximum(*carries))

    def step(j, strict, rows=slice(0, SB_BLOCK)):
        return functools.reduce(jnp.maximum, [pair_step(p, j, strict, rows) for p in range(npairs)])

    def alive(top):
        return top > -SB_UNDERFLOW_LOG2

    carry_ref[...] = jnp.zeros_like(carry_ref)
    acc_ref[...] = jnp.zeros_like(acc_ref)
    qpos = lax.broadcasted_iota(jnp.int32, (tq, tq), 0)
    kpos = lax.broadcasted_iota(jnp.int32, (tq, tq), 1)
    strict = kpos < qpos

    def first_two():
        step(i, strict)
        top = step(i - 1, None, slice(0, SB_NEIGHBOUR_ROWS))
        tail = slice(SB_NEIGHBOUR_ROWS, tq)
        tail_top = jnp.max(carry_ref[:, tail, :])
        tail_top = lax.cond(alive(tail_top), lambda: step(i - 1, None, tail), lambda: tail_top)
        return jnp.maximum(top, tail_top)

    top = lax.cond(i > 0, first_two, lambda: step(i, strict))

    def live(state):
        j, top = state
        return jnp.logical_and(j >= 0, alive(top))

    def body(state):
        j, _ = state
        return j - 1, step(j, None)

    lax.while_loop(live, body, (i - 2, top))
    for p in range(npairs):
        o_ref[:, p * LANES:(p + 1) * LANES] = acc_ref[p].astype(o_ref.dtype)


def _sb_attention(qkv, batch, seq):
    tq = SB_BLOCK
    nq = seq // tq
    width = SB_PAIRS_PER_STEP * LANES
    nsteps = D_MODEL // width
    return pl.pallas_call(
        _sb_kernel,
        grid=(batch, nsteps, nq),
        in_specs=[pl.BlockSpec((tq, width), lambda b, p, i: (b * nq + i, p)),
                  pl.BlockSpec((seq, width), lambda b, p, i: (b, nsteps + p)),
                  pl.BlockSpec((seq, width), lambda b, p, i: (b, 2 * nsteps + p))],
        out_specs=pl.BlockSpec((tq, width), lambda b, p, i: (b * nq + i, p)),
        out_shape=jax.ShapeDtypeStruct((batch * seq, D_MODEL), BF16),
        scratch_shapes=[pltpu.VMEM((2 * SB_PAIRS_PER_STEP, tq, LANES), F32),
                        pltpu.VMEM((SB_PAIRS_PER_STEP, tq, LANES), F32)],
        compiler_params=_params("arbitrary", "arbitrary", "arbitrary"),
        name="sb_attention",
    )(qkv, qkv, qkv)


def kernel(x, mix_norm, ffn_norm, final_norm, ssd_w_in, ssd_conv_w, ssd_conv_b, ssd_dt_bias, ssd_a_log, ssd_d, ssd_norm, ssd_w_out, sb_w_qkv, sb_w_out, ffn_w_in, ffn_conv_w, ffn_conv_b, ffn_w_out):
    batch, seq, d = x.shape
    xf = x.reshape(batch * seq, d)
    h = _rmsnorm(xf, mix_norm[0])
    out = None
    for i in range(DEPTH):
        j = i // 2
        if i % 2 == 0:
            w_dt = jnp.pad(ssd_w_in[j, :, SSD_D_INNER + SSD_CONV_DIM:], ((0, 0), (0, LANES - SSD_HEADS)))[None]
            z = _mm(h, ssd_w_in, j, 0, SSD_D_INNER, BF16, *MM_TILE, "ssd_z")
            xbc = _mm_conv_act(h, ssd_w_in, j, SSD_D_INNER, ssd_conv_w[j], ssd_conv_b[j], 1, seq, *XBC_TILE, "ssd_xbc")
            dt_raw = _mm(h, w_dt, 0, 0, LANES, F32, MM_TILE[0], LANES, "ssd_dt")
            mixed = _ssd_scan(z, xbc, dt_raw, ssd_dt_bias[j], ssd_a_log[j], ssd_d[j], ssd_norm[j], batch, seq)
            w_out = ssd_w_out
        else:
            qkv = _mm(h, sb_w_qkv, j, 0, 3 * d, BF16, *MM_TILE, "sb_qkv",
                      scaled_tiles=d // MM_TILE[1], scale=LOG2E * SB_HEAD_DIM ** -0.5)
            mixed = _sb_attention(qkv, batch, seq)
            w_out = sb_w_out
        xf, h = _mm_res_norm(mixed, w_out, j, xf, ffn_norm[i], False, OUT_ROW_TILE, "mixer_out")
        act = _mm_conv_act(h, ffn_w_in, i, 0, ffn_conv_w[i], ffn_conv_b[i], 2, seq, *FFN_IN_TILE, "ffn_in")
        if i + 1 < DEPTH:
            xf, h = _mm_res_norm(act, ffn_w_out, i, xf, mix_norm[i + 1], False, OUT_ROW_TILE, "ffn_out")
        else:
            out = _mm_res_norm(act, ffn_w_out, i, xf, final_norm, True, OUT_ROW_TILE, "ffn_out_final")
    return out.reshape(batch, seq, d)
```

```python
import functools
import math

import jax
import jax.numpy as jnp
from jax import lax
from jax.experimental import pallas as pl
from jax.experimental.pallas import tpu as pltpu

F32 = jnp.float32
BF16 = jnp.bfloat16

D_MODEL = 1024
DEPTH = 4
NORM_EPS = 1e-6
LOG2E = math.log2(math.e)

SSD_D_INNER = 2048
SSD_HEAD_DIM = 64
SSD_HEADS = 32
SSD_GROUPS = 8
SSD_HEADS_PER_GROUP = 4
SSD_STATE = 128
SSD_CHUNK = 128
SSD_CHUNKS_PER_STEP = 4
SSD_CONV_DIM = SSD_D_INNER + 2 * SSD_GROUPS * SSD_STATE
SSD_GROUP_WIDTH = SSD_HEADS_PER_GROUP * SSD_HEAD_DIM

SB_HEADS = 16
SB_HEAD_DIM = 64
SB_BLOCK = 256
SB_PAIRS_PER_STEP = 8
SB_UNDERFLOW_LOG2 = 160.0
SB_NEIGHBOUR_ROWS = 192

FFN_D_FF = 2816

MM_TILE = (2048, 1024)
XBC_TILE = (512, 2048)
CONV_CHUNK = 256
CONV_SLAB = 64
FFN_IN_TILE = (512, FFN_D_FF // 2)
OUT_ROW_TILE = 1024

LANES = 128
SUBLANES = 8
VMEM_LIMIT_BYTES = 56 * 1024 * 1024


def _params(*semantics):
    return pltpu.CompilerParams(dimension_semantics=semantics, vmem_limit_bytes=VMEM_LIMIT_BYTES)


def _rms_scale(v):
    return lax.rsqrt(jnp.mean(v * v, axis=-1, keepdims=True) + NORM_EPS)


def _rmsnorm_kernel(x_ref, g_ref, o_ref):
    x = x_ref[...]
    o_ref[...] = (x * _rms_scale(x) * g_ref[...]).astype(o_ref.dtype)


def _rmsnorm(x, g, tm=1024):
    t, d = x.shape
    return pl.pallas_call(
        _rmsnorm_kernel,
        grid=(t // tm,),
        in_specs=[pl.BlockSpec((tm, d), lambda i: (i, 0)), pl.BlockSpec((1, d), lambda i: (0, 0))],
        out_specs=pl.BlockSpec((tm, d), lambda i: (i, 0)),
        out_shape=jax.ShapeDtypeStruct((t, d), BF16),
        compiler_params=_params("arbitrary"),
        name="rmsnorm",
    )(x, g.reshape(1, d))


def _mm_kernel(h_ref, w_ref, o_ref, wb_ref, *, scaled_tiles, scale):
    @pl.when(pl.program_id(1) == 0)
    def _():
        w = w_ref[...]
        if scaled_tiles:
            w = w * jnp.where(pl.program_id(0) < scaled_tiles, scale, 1.0)
        wb_ref[...] = w.astype(BF16)

    o_ref[...] = jnp.dot(h_ref[...], wb_ref[...], preferred_element_type=F32).astype(o_ref.dtype)


def _mm(h, w, layer, col0, n, out_dtype, tm, tn, name, scaled_tiles=0, scale=1.0):
    t, k = h.shape
    off = col0 // tn
    return pl.pallas_call(
        functools.partial(_mm_kernel, scaled_tiles=scaled_tiles, scale=scale),
        grid=(n // tn, t // tm),
        in_specs=[pl.BlockSpec((tm, k), lambda j, i: (i, 0)),
                  pl.BlockSpec((None, k, tn), lambda j, i: (layer, 0, j + off), pipeline_mode=pl.Buffered(1))],
        out_specs=pl.BlockSpec((tm, tn), lambda j, i: (i, j)),
        out_shape=jax.ShapeDtypeStruct((t, n), out_dtype),
        scratch_shapes=[pltpu.VMEM((k, tn), BF16)],
        compiler_params=_params("arbitrary", "arbitrary"),
        name=name,
    )(h, w)


def _causal_conv(u3, prev, cw, cb):
    width = cw.shape[0]
    n = u3.shape[2]
    before = jnp.concatenate([prev, u3[:-1]], axis=0)
    sub = lax.broadcasted_iota(jnp.int32, (1, SUBLANES, n), 1)
    y = cb.reshape(1, 1, n) + cw[width - 1:width, :].reshape(1, 1, n) * u3
    for j in range(1, width):
        mixed = jnp.where(sub >= SUBLANES - j, before, u3)
        y = y + cw[width - 1 - j:width - j, :].reshape(1, 1, n) * pltpu.roll(mixed, j, axis=1)
    return y


def _silu(y):
    return y * (1.0 / (1.0 + jnp.exp2(y * -LOG2E)))


def _mm_conv_act_kernel(*refs, nbranch, tiles_per_seq):
    h_ref = refs[0]
    w_refs = refs[1:1 + nbranch]
    cw_refs = refs[1 + nbranch:1 + 2 * nbranch]
    cb_refs = refs[1 + 2 * nbranch:1 + 3 * nbranch]
    o_ref = refs[1 + 3 * nbranch]
    prev_refs = refs[2 + 3 * nbranch:2 + 4 * nbranch]
    wb_ref = refs[2 + 4 * nbranch]
    tm = h_ref.shape[0]
    tn = o_ref.shape[1]

    @pl.when(pl.program_id(1) == 0)
    def _():
        for b, w_ref in enumerate(w_refs):
            wb_ref[:, b * tn:(b + 1) * tn] = w_ref[...].astype(BF16)

    @pl.when(pl.program_id(1) % tiles_per_seq == 0)
    def _():
        for prev_ref in prev_refs:
            prev_ref[...] = jnp.zeros_like(prev_ref)

    u_all = jnp.dot(h_ref[...], wb_ref[...], preferred_element_type=F32)
    ys = []
    for b in range(nbranch):
        u = u_all[:, b * tn:(b + 1) * tn]
        u3 = u.reshape(tm // SUBLANES, SUBLANES, u.shape[1])
        ys.append(_causal_conv(u3, prev_refs[b][...], cw_refs[b][...], cb_refs[b][...]))
        prev_refs[b][...] = u3[tm // SUBLANES - 1:]
    act = _silu(ys[0])
    if nbranch == 2:
        act = act * ys[1]
    o_ref[...] = act.reshape(o_ref.shape).astype(o_ref.dtype)


def _mm_conv_act(h, w, layer, col0, cw, cb, nbranch, seq, tm, tn, name):
    t, k = h.shape
    n_out = cw.shape[1] // nbranch
    width = cw.shape[0]
    ncol = n_out // tn
    off = col0 // tn
    in_specs = [pl.BlockSpec((tm, k), lambda j, i: (i, 0))]
    in_specs += [pl.BlockSpec((None, k, tn), lambda j, i, b=b: (layer, 0, j + b * ncol + off),
                              pipeline_mode=pl.Buffered(1)) for b in range(nbranch)]
    in_specs += [pl.BlockSpec((width, tn), lambda j, i, b=b: (0, j + b * ncol)) for b in range(nbranch)]
    in_specs += [pl.BlockSpec((1, tn), lambda j, i, b=b: (0, j + b * ncol)) for b in range(nbranch)]
    cb2 = cb.reshape(1, -1)
    return pl.pallas_call(
        functools.partial(_mm_conv_act_kernel, nbranch=nbranch, tiles_per_seq=seq // tm),
        grid=(ncol, t // tm),
        in_specs=in_specs,
        out_specs=pl.BlockSpec((tm, tn), lambda j, i: (i, j)),
        out_shape=jax.ShapeDtypeStruct((t, n_out), BF16),
        scratch_shapes=[pltpu.VMEM((1, SUBLANES, tn), F32)] * nbranch + [pltpu.VMEM((k, nbranch * tn), BF16)],
        compiler_params=_params("arbitrary", "arbitrary"),
        name=name,
    )(h, *([w] * nbranch), *([cw] * nbranch), *([cb2] * nbranch))


def _mm_conv_staged_kernel(h_ref, w_ref, cw_ref, cb_ref, o_ref, wb_ref, *u_refs, tiles_per_seq):
    tm = h_ref.shape[0]
    i = pl.program_id(1)

    @pl.when(i == 0)
    def _():
        wb_ref[...] = w_ref[...].astype(BF16)

    @pl.when(i % tiles_per_seq == 0)
    def _():
        for u_ref in u_refs:
            u_ref[0:SUBLANES, :] = jnp.zeros((SUBLANES, u_ref.shape[1]), F32)

    h = h_ref[...]
    for c, u_ref in enumerate(u_refs):
        u_ref[SUBLANES:SUBLANES + tm, :] = jnp.dot(h, wb_ref[:, c * CONV_CHUNK:(c + 1) * CONV_CHUNK],
                                                   preferred_element_type=F32)

    zero = pl.multiple_of(jnp.minimum(i, 0) * SUBLANES, SUBLANES)
    groups = CONV_SLAB // SUBLANES
    for c, u_ref in enumerate(u_refs):
        cols = slice(c * CONV_CHUNK, (c + 1) * CONV_CHUNK)
        for r0 in range(0, tm, CONV_SLAB):
            rows = u_ref[pl.ds(zero + r0, CONV_SLAB + SUBLANES), :].reshape(groups + 1, SUBLANES, CONV_CHUNK)
            y = _causal_conv(rows[1:], rows[:1], cw_ref[:, cols], cb_ref[:, cols])
            o_ref[r0:r0 + CONV_SLAB, cols] = _silu(y).reshape(CONV_SLAB, CONV_CHUNK).astype(o_ref.dtype)
        u_ref[0:SUBLANES, :] = u_ref[pl.ds(zero + tm, SUBLANES), :]


def _mm_conv_staged(h, w, layer, col0, cw, cb, seq, tm, tn, name):
    t, k = h.shape
    n_out = cw.shape[1]
    width = cw.shape[0]
    off = col0 // tn
    return pl.pallas_call(
        functools.partial(_mm_conv_staged_kernel, tiles_per_seq=seq // tm),
        grid=(n_out // tn, t // tm),
        in_specs=[pl.BlockSpec((tm, k), lambda j, i: (i, 0)),
                  pl.BlockSpec((None, k, tn), lambda j, i: (layer, 0, j + off), pipeline_mode=pl.Buffered(1)),
                  pl.BlockSpec((width, tn), lambda j, i: (0, j)),
                  pl.BlockSpec((1, tn), lambda j, i: (0, j))],
        out_specs=pl.BlockSpec((tm, tn), lambda j, i: (i, j)),
        out_shape=jax.ShapeDtypeStruct((t, n_out), BF16),
        scratch_shapes=[pltpu.VMEM((k, tn), BF16)] + [pltpu.VMEM((tm + SUBLANES, CONV_CHUNK), F32)] * (tn // CONV_CHUNK),
        compiler_params=_params("arbitrary", "arbitrary"),
        name=name,
    )(h, w, cw, cb.reshape(1, -1))


def _residual_update(a_ref, w_ref, x_ref, wb_ref):
    @pl.when(pl.program_id(0) == 0)
    def _():
        wb_ref[...] = w_ref[...].astype(BF16)

    return x_ref[...] + jnp.dot(a_ref[...], wb_ref[...], preferred_element_type=F32)


def _mm_res_norm_kernel(a_ref, w_ref, x_ref, g_ref, xo_ref, ho_ref, wb_ref):
    xn = _residual_update(a_ref, w_ref, x_ref, wb_ref)
    xo_ref[...] = xn
    ho_ref[...] = (xn * _rms_scale(xn) * g_ref[...]).astype(ho_ref.dtype)


def _mm_res_final_kernel(a_ref, w_ref, x_ref, g_ref, o_ref, wb_ref):
    xn = _residual_update(a_ref, w_ref, x_ref, wb_ref)
    o_ref[...] = xn * _rms_scale(xn) * g_ref[...]


def _mm_res_norm(a, w, layer, x, g, final, tm, name):
    t, k = a.shape
    d = w.shape[2]
    row = pl.BlockSpec((tm, d), lambda i: (i, 0))
    in_specs = [pl.BlockSpec((tm, k), lambda i: (i, 0)),
                pl.BlockSpec((None, k, d), lambda i: (layer, 0, 0), pipeline_mode=pl.Buffered(1)),
                row, pl.BlockSpec((1, d), lambda i: (0, 0))]
    scratch = [pltpu.VMEM((k, d), BF16)]
    if final:
        return pl.pallas_call(
            _mm_res_final_kernel, grid=(t // tm,), in_specs=in_specs, out_specs=row,
            out_shape=jax.ShapeDtypeStruct((t, d), F32), scratch_shapes=scratch,
            compiler_params=_params("arbitrary"), name=name,
        )(a, w, x, g.reshape(1, d))
    return pl.pallas_call(
        _mm_res_norm_kernel, grid=(t // tm,), in_specs=in_specs, out_specs=[row, row],
        out_shape=[jax.ShapeDtypeStruct((t, d), F32), jax.ShapeDtypeStruct((t, d), BF16)], scratch_shapes=scratch,
        compiler_params=_params("arbitrary"), name=name,
    )(a, w, x, g.reshape(1, d))


def _split3(v):
    a = v.astype(BF16)
    r = v - a.astype(F32)
    b = r.astype(BF16)
    c = (r - b.astype(F32)).astype(BF16)
    return a, b, c


def _expand_heads(v, h0, nheads):
    rows = v.shape[0]
    lane = lax.broadcasted_iota(jnp.int32, (rows, LANES), 1)
    pairs = []
    for p in range(nheads // 2):
        if rows % SUBLANES:
            lo = jnp.broadcast_to(v[:, h0 + 2 * p:h0 + 2 * p + 1], (rows, LANES))
            hi = jnp.broadcast_to(v[:, h0 + 2 * p + 1:h0 + 2 * p + 2], (rows, LANES))
            pairs.append(jnp.where(lane < SSD_HEAD_DIM, lo, hi))
        else:
            pairs.append(jnp.take_along_axis(v, h0 + 2 * p + lane // SSD_HEAD_DIM, axis=1))
    return jnp.concatenate(pairs, axis=1)


def _ssd_kernel(z_ref, xbc_ref, dt_ref, dtb_ref, alog_ref, dskip_ref, ng_ref, o_ref, state_ref, y_ref):
    @pl.when(pl.program_id(1) == 0)
    def _():
        state_ref[...] = jnp.zeros_like(state_ref)

    for c in range(SSD_CHUNKS_PER_STEP):
        _ssd_chunk(slice(c * SSD_CHUNK, (c + 1) * SSD_CHUNK), z_ref, xbc_ref, dt_ref, dtb_ref, alog_ref, dskip_ref,
                   ng_ref, o_ref, state_ref, y_ref)


def _ssd_chunk(rows, z_ref, xbc_ref, dt_ref, dtb_ref, alog_ref, dskip_ref, ng_ref, o_ref, state_ref, y_ref):
    L = SSD_CHUNK
    pre = dt_ref[rows, :] + dtb_ref[...]
    dt = jnp.maximum(pre, 0.0) + jnp.log1p(jnp.exp(-jnp.abs(pre)))
    a = (-LOG2E * jnp.exp(alog_ref[...])) * dt
    row = lax.broadcasted_iota(jnp.int32, (L, L), 0)
    col = lax.broadcasted_iota(jnp.int32, (L, L), 1)
    causal = row >= col
    tri = jnp.where(causal, 1.0, 0.0).astype(BF16)
    a1, a2, a3 = _split3(a)
    acum = (jnp.dot(tri, a1, preferred_element_type=F32) + jnp.dot(tri, a2, preferred_element_type=F32)
            + jnp.dot(tri, a3, preferred_element_type=F32))
    src_t = (acum - jnp.log2(jnp.maximum(dt, jnp.finfo(F32).tiny))).T
    last = acum[L - 1:L, :]
    exp_acum = jnp.exp2(acum)
    w_dt = jnp.exp2(last - acum) * dt
    exp_last = jnp.exp2(last)
    lane = lax.broadcasted_iota(jnp.int32, (L, LANES), 1)
    lo_lanes = lane < SSD_HEAD_DIM

    for g in range(SSD_GROUPS):
        b_g = xbc_ref[rows, SSD_D_INNER + g * SSD_STATE:SSD_D_INNER + (g + 1) * SSD_STATE]
        c_off = SSD_D_INNER + SSD_GROUPS * SSD_STATE
        c_g = xbc_ref[rows, c_off + g * SSD_STATE:c_off + (g + 1) * SSD_STATE]
        cb = lax.dot_general(c_g, b_g, (((1,), (1,)), ((), ())), preferred_element_type=F32)
        x0 = g * SSD_GROUP_WIDTH
        h0 = g * SSD_HEADS_PER_GROUP
        y_diag = []
        for p in range(SSD_HEADS_PER_GROUP // 2):
            scores = []
            for hh in (h0 + 2 * p, h0 + 2 * p + 1):
                seg = acum[:, hh:hh + 1] - src_t[hh:hh + 1, :]
                scores.append((cb * jnp.exp2(jnp.where(causal, seg, -jnp.inf))).astype(BF16))
            xs = xbc_ref[rows, x0 + p * LANES:x0 + (p + 1) * LANES]
            zero = jnp.zeros_like(xs)
            xs_split = jnp.concatenate([jnp.where(lo_lanes, xs, zero), jnp.where(lo_lanes, zero, xs)], axis=0)
            y_diag.append(jnp.dot(jnp.concatenate(scores, axis=1), xs_split, preferred_element_type=F32))
        y_diag = jnp.concatenate(y_diag, axis=1)

        xs_g = xbc_ref[rows, x0:x0 + SSD_GROUP_WIDTH].astype(F32)
        state = state_ref[g]
        y_off = (jnp.dot(c_g, state.astype(BF16), preferred_element_type=F32)
                 * _expand_heads(exp_acum, h0, SSD_HEADS_PER_GROUP))
        xw = (xs_g * _expand_heads(w_dt, h0, SSD_HEADS_PER_GROUP)).astype(BF16)
        state_ref[g] = (state * _expand_heads(exp_last, h0, SSD_HEADS_PER_GROUP)
                        + lax.dot_general(b_g, xw, (((0,), (0,)), ((), ())), preferred_element_type=F32))
        y_ref[rows, x0:x0 + SSD_GROUP_WIDTH] = y_diag + y_off + dskip_ref[:, x0:x0 + SSD_GROUP_WIDTH] * xs_g

    z = z_ref[rows, :].astype(F32)
    gated = y_ref[rows, :] * _silu(z)
    o_ref[rows, :] = (gated * _rms_scale(gated) * ng_ref[...]).astype(o_ref.dtype)


def _ssd_scan(z, xbc, dt_raw, dt_bias, a_log, d_skip, norm_g, batch, seq):
    L = SSD_CHUNKS_PER_STEP * SSD_CHUNK
    nc = seq // L
    pad = LANES - SSD_HEADS
    dtb = jnp.pad(dt_bias, (0, pad)).reshape(1, LANES)
    alog = jnp.pad(a_log, (0, pad)).reshape(1, LANES)
    dskip = jnp.repeat(d_skip, SSD_HEAD_DIM).reshape(1, SSD_D_INNER)
    small = lambda n: pl.BlockSpec((1, n), lambda b, c: (0, 0))
    return pl.pallas_call(
        _ssd_kernel,
        grid=(batch, nc),
        in_specs=[pl.BlockSpec((L, SSD_D_INNER), lambda b, c: (b * nc + c, 0)),
                  pl.BlockSpec((L, SSD_CONV_DIM), lambda b, c: (b * nc + c, 0)),
                  pl.BlockSpec((L, LANES), lambda b, c: (b * nc + c, 0)),
                  small(LANES), small(LANES), small(SSD_D_INNER), small(SSD_D_INNER)],
        out_specs=pl.BlockSpec((L, SSD_D_INNER), lambda b, c: (b * nc + c, 0)),
        out_shape=jax.ShapeDtypeStruct((batch * seq, SSD_D_INNER), BF16),
        scratch_shapes=[pltpu.VMEM((SSD_GROUPS, SSD_STATE, SSD_GROUP_WIDTH), F32),
                        pltpu.VMEM((L, SSD_D_INNER), F32)],
        compiler_params=_params("arbitrary", "arbitrary"),
        name="ssd_scan",
    )(z, xbc, dt_raw, dtb, alog, dskip, norm_g.reshape(1, SSD_D_INNER))


def _sb_head_block(qh, kb, carry, suffix_mat, strict):
    tk = kb.shape[0]
    l = lax.dot_general(qh, kb, (((1,), (1,)), ((), ())), preferred_element_type=F32)
    neg_abs = lax.bitcast_convert_type(lax.bitcast_convert_type(l, jnp.uint32) | jnp.uint32(0x80000000), F32)
    lg = jnp.log2(1.0 + jnp.exp2(neg_abs))
    log_beta = jnp.minimum(l, 0.0) - lg
    log_fail = log_beta - l
    if strict is not None:
        log_fail = jnp.where(strict, log_fail, 0.0)
    inner = jnp.dot(log_fail.astype(BF16), suffix_mat, preferred_element_type=F32)
    w = jnp.exp2(log_beta + inner + jnp.concatenate([carry] * (tk // LANES), axis=1))
    if strict is not None:
        w = jnp.where(strict, w, 0.0)
    return w.astype(BF16), carry + jnp.sum(log_fail, axis=-1, keepdims=True)


def _sb_kernel(q_ref, k_ref, v_ref, o_ref, carry_ref, acc_ref):
    tq = SB_BLOCK
    i = pl.program_id(2)
    npairs = q_ref.shape[1] // LANES
    suffix_mat = jnp.where(lax.broadcasted_iota(jnp.int32, (tq, tq), 0) > lax.broadcasted_iota(jnp.int32, (tq, tq), 1),
                           1.0, 0.0).astype(BF16)
    lo_lanes = lax.broadcasted_iota(jnp.int32, (tq, LANES), 1) < SB_HEAD_DIM

    def pair_step(p, j, strict, rows=slice(0, SB_BLOCK)):
        start = pl.multiple_of(j * tq, tq)
        lanes = slice(p * LANES, (p + 1) * LANES)
        q = q_ref[rows, lanes]
        kb = k_ref[pl.ds(start, tq), lanes]
        vb = v_ref[pl.ds(start, tq), lanes]
        lo_q = lax.broadcasted_iota(jnp.int32, q.shape, 1) < SB_HEAD_DIM
        v_split = jnp.concatenate([jnp.where(lo_lanes, vb, jnp.zeros_like(vb)),
                                   jnp.where(lo_lanes, jnp.zeros_like(vb), vb)], axis=0)
        ws, carries = [], []
        for hh, qh in enumerate((jnp.where(lo_q, q, jnp.zeros_like(q)), jnp.where(lo_q, jnp.zeros_like(q), q))):
            w, carry = _sb_head_block(qh, kb, carry_ref[2 * p + hh, rows], suffix_mat, strict)
            carry_ref[2 * p + hh, rows] = carry
            ws.append(w)
            carries.append(carry)
        acc_ref[p, rows] += jnp.dot(jnp.concatenate(ws, axis=1), v_split, preferred_element_type=F32)
        return jnp.max(jnp.maximum(*carries))

    def step(j, strict, rows=slice(0, SB_BLOCK)):
        return functools.reduce(jnp.maximum, [pair_step(p, j, strict, rows) for p in range(npairs)])

    def alive(top):
        return top > -SB_UNDERFLOW_LOG2

    carry_ref[...] = jnp.zeros_like(carry_ref)
    acc_ref[...] = jnp.zeros_like(acc_ref)
    qpos = lax.broadcasted_iota(jnp.int32, (tq, tq), 0)
    kpos = lax.broadcasted_iota(jnp.int32, (tq, tq), 1)
    strict = kpos < qpos

    def first_two():
        step(i, strict)
        top = step(i - 1, None, slice(0, SB_NEIGHBOUR_ROWS))
        tail = slice(SB_NEIGHBOUR_ROWS, tq)
        tail_top = jnp.max(carry_ref[:, tail, :])
        tail_top = lax.cond(alive(tail_top), lambda: step(i - 1, None, tail), lambda: tail_top)
        return jnp.maximum(top, tail_top)

    top = lax.cond(i > 0, first_two, lambda: step(i, strict))

    def live(state):
        j, top = state
        return jnp.logical_and(j >= 0, alive(top))

    def body(state):
        j, _ = state
        return j - 1, step(j, None)

    lax.while_loop(live, body, (i - 2, top))
    for p in range(npairs):
        o_ref[:, p * LANES:(p + 1) * LANES] = acc_ref[p].astype(o_ref.dtype)


def _sb_attention(qkv, batch, seq):
    tq = SB_BLOCK
    nq = seq // tq
    width = SB_PAIRS_PER_STEP * LANES
    nsteps = D_MODEL // width
    return pl.pallas_call(
        _sb_kernel,
        grid=(batch, nsteps, nq),
        in_specs=[pl.BlockSpec((tq, width), lambda b, p, i: (b * nq + i, p)),
                  pl.BlockSpec((seq, width), lambda b, p, i: (b, nsteps + p)),
                  pl.BlockSpec((seq, width), lambda b, p, i: (b, 2 * nsteps + p))],
        out_specs=pl.BlockSpec((tq, width), lambda b, p, i: (b * nq + i, p)),
        out_shape=jax.ShapeDtypeStruct((batch * seq, D_MODEL), BF16),
        scratch_shapes=[pltpu.VMEM((2 * SB_PAIRS_PER_STEP, tq, LANES), F32),
                        pltpu.VMEM((SB_PAIRS_PER_STEP, tq, LANES), F32)],
        compiler_params=_params("arbitrary", "arbitrary", "arbitrary"),
        name="sb_attention",
    )(qkv, qkv, qkv)


def kernel(x, mix_norm, ffn_norm, final_norm, ssd_w_in, ssd_conv_w, ssd_conv_b, ssd_dt_bias, ssd_a_log, ssd_d, ssd_norm, ssd_w_out, sb_w_qkv, sb_w_out, ffn_w_in, ffn_conv_w, ffn_conv_b, ffn_w_out):
    batch, seq, d = x.shape
    xf = x.reshape(batch * seq, d)
    h = _rmsnorm(xf, mix_norm[0])
    out = None
    for i in range(DEPTH):
        j = i // 2
        if i % 2 == 0:
            w_dt = jnp.pad(ssd_w_in[j, :, SSD_D_INNER + SSD_CONV_DIM:], ((0, 0), (0, LANES - SSD_HEADS)))[None]
            z = _mm(h, ssd_w_in, j, 0, SSD_D_INNER, BF16, *MM_TILE, "ssd_z")
            xbc = _mm_conv_staged(h, ssd_w_in, j, SSD_D_INNER, ssd_conv_w[j], ssd_conv_b[j], seq, *XBC_TILE, "ssd_xbc")
            dt_raw = _mm(h, w_dt, 0, 0, LANES, F32, MM_TILE[0], LANES, "ssd_dt")
            mixed = _ssd_scan(z, xbc, dt_raw, ssd_dt_bias[j], ssd_a_log[j], ssd_d[j], ssd_norm[j], batch, seq)
            w_out = ssd_w_out
        else:
            qkv = _mm(h, sb_w_qkv, j, 0, 3 * d, BF16, *MM_TILE, "sb_qkv",
                      scaled_tiles=d // MM_TILE[1], scale=LOG2E * SB_HEAD_DIM ** -0.5)
            mixed = _sb_attention(qkv, batch, seq)
            w_out = sb_w_out
        xf, h = _mm_res_norm(mixed, w_out, j, xf, ffn_norm[i], False, OUT_ROW_TILE, "mixer_out")
        act = _mm_conv_act(h, ffn_w_in, i, 0, ffn_conv_w[i], ffn_conv_b[i], 2, seq, *FFN_IN_TILE, "ffn_in")
        if i + 1 < DEPTH:
            xf, h = _mm_res_norm(act, ffn_w_out, i, xf, mix_norm[i + 1], False, OUT_ROW_TILE, "ffn_out")
        else:
            out = _mm_res_norm(act, ffn_w_out, i, xf, final_norm, True, OUT_ROW_TILE, "ffn_out_final")
    return out.reshape(batch, seq, d)
```
